```python
import jax, jax.numpy as jnp
from jax import lax
import numpy as np

D_MODEL = 1024
BATCH = 16
SEQ = 4096
DEPTH = 1

D_MIX = D_MODEL
D_CONF = D_MIX // 2
D_SC = D_MIX - D_CONF
CONF_HEADS = 8
SC_HEADS = 8
CONF_KERNEL = 31
SC_KERNEL = 3
D_FF = 4 * D_MODEL
D_IN = 2 * D_CONF + 3 * D_SC
N_MOD = 6
EPS = 1e-6

kernel_name = "hybrid_conformer_shortconv_adaln_block"


def rms_norm(x, gain=None):
    xf = x.astype(jnp.float32)
    y = xf * lax.rsqrt(jnp.mean(xf * xf, axis=-1, keepdims=True) + EPS)
    if gain is not None:
        y = y * gain.astype(jnp.float32)
    return y.astype(x.dtype)


def layer_norm(x, gain, bias):
    xf = x.astype(jnp.float32)
    mu = jnp.mean(xf, axis=-1, keepdims=True)
    var = jnp.mean(jnp.square(xf - mu), axis=-1, keepdims=True)
    y = (xf - mu) * lax.rsqrt(var + EPS) * gain.astype(jnp.float32) + bias.astype(jnp.float32)
    return y.astype(x.dtype)


def causal_depthwise_conv(u, w):
    k = w.shape[0]
    return lax.conv_general_dilated(
        u, w[:, None, :].astype(u.dtype), window_strides=(1,), padding=[(k - 1, 0)],
        dimension_numbers=("NWC", "WIO", "NWC"), feature_group_count=u.shape[-1])


def modulate(h, shift, scale):
    return h * (1.0 + scale[:, None, :]) + shift[:, None, :]


def setup_inputs(seed: int = 0) -> dict:
    key = jax.random.key(seed)
    ks = jax.random.split(key, 16)
    f = jnp.float32
    n = lambda k, shape, s: (jax.random.normal(k, shape, f) * s).astype(f)
    return {
        "x": n(ks[0], (BATCH, SEQ, D_MODEL), 1.0),
        "c": n(ks[1], (BATCH, D_MODEL), 1.0),
        "w_ada": n(ks[2], (DEPTH, D_MODEL, N_MOD * D_MODEL), 0.5 * D_MODEL ** -0.5),
        "b_ada": n(ks[3], (DEPTH, N_MOD * D_MODEL), 0.02),
        "w_in": n(ks[4], (DEPTH, D_MODEL, D_IN), D_MODEL ** -0.5),
        "conf_dw_w": n(ks[5], (DEPTH, CONF_KERNEL, D_CONF), CONF_KERNEL ** -0.5),
        "conf_dw_b": n(ks[6], (DEPTH, D_CONF), 0.02),
        "conf_ln_g": 1.0 + n(ks[7], (DEPTH, D_CONF), 0.02),
        "conf_ln_b": n(ks[8], (DEPTH, D_CONF), 0.02),
        "sc_conv_w": n(ks[9], (DEPTH, SC_KERNEL, D_SC), SC_KERNEL ** -0.5),
        "w_out": n(ks[10], (DEPTH, D_MIX, D_MODEL), D_MIX ** -0.5),
        "w_mlp1": n(ks[11], (DEPTH, D_MODEL, D_FF), D_MODEL ** -0.5),
        "w_mlp2": n(ks[12], (DEPTH, D_FF, D_MODEL), D_FF ** -0.5),
        "g_final": 1.0 + n(ks[13], (D_MODEL,), 0.02),
    }


def reference(x, c, w_ada, b_ada, w_in, conf_dw_w, conf_dw_b, conf_ln_g, conf_ln_b,
              sc_conv_w, w_out, w_mlp1, w_mlp2, g_final):
    dt = x.dtype
    c_act = jax.nn.silu(c)
    for l in range(DEPTH):
        mod = jnp.einsum("bd,de->be", c_act, w_ada[l]) + b_ada[l]
        sh1, sc1, g1, sh2, sc2, g2 = jnp.split(mod.astype(dt), N_MOD, axis=-1)

        h = modulate(rms_norm(x), sh1, sc1)
        proj = jnp.einsum("btd,de->bte", h, w_in[l])
        conf_val, conf_gate, sc_b, sc_c, sc_h = jnp.split(
            proj, np.cumsum([D_CONF, D_CONF, D_SC, D_SC])[:4].tolist(), axis=-1)

        a = conf_val * jax.nn.sigmoid(conf_gate)
        a = causal_depthwise_conv(a, conf_dw_w[l]) + conf_dw_b[l].astype(dt)
        a = jax.nn.silu(layer_norm(a, conf_ln_g[l], conf_ln_b[l]))

        s = sc_b * causal_depthwise_conv(sc_c * sc_h, sc_conv_w[l])

        mixed = jnp.concatenate([a, s], axis=-1)
        y = jnp.einsum("bte,ed->btd", mixed, w_out[l])
        x = x + g1[:, None, :] * y

        h = modulate(rms_norm(x), sh2, sc2)
        u = jnp.square(jax.nn.relu(jnp.einsum("btd,df->btf", h, w_mlp1[l])))
        y = jnp.einsum("btf,fd->btd", u, w_mlp2[l])
        x = x + g2[:, None, :] * y

    return rms_norm(x, g_final)
```

```python
import jax
import jax.numpy as jnp
from jax import lax
from jax.experimental import pallas as pl
from jax.experimental.pallas import tpu as pltpu

EPS = 1e-6
N_MOD = 6
CONF_KERNEL = 31
SC_KERNEL = 3

SUBLANES_F32 = 8
LANES = 128
VMEM_LIMIT_BYTES = 56 * 1024 * 1024

TIME_TILE = 512
CONV_ROWS = 64
FF_CHUNK = 1024
A_HALO = -(-(CONF_KERNEL - 1) // SUBLANES_F32) * SUBLANES_F32
Q_HALO = -(-(SC_KERNEL - 1) // SUBLANES_F32) * SUBLANES_F32


def _dot(a, b):
    return jnp.dot(a, b, preferred_element_type=jnp.float32)


def _rms_scale(v):
    return lax.rsqrt(jnp.mean(v * v, axis=-1, keepdims=True) + EPS)


def _rows(tile, rows):
    n = tile.shape[-1]
    return jnp.broadcast_to(tile[None], (rows // SUBLANES_F32, SUBLANES_F32, n)).reshape(rows, n)


def _sublane_replicate(p):
    k, n = p.shape
    return jnp.broadcast_to(p[:, None, :], (k, SUBLANES_F32, n)).reshape(k * SUBLANES_F32, n)


def _causal_taps(win, w_ref, lanes, halo, ksize):
    total = win.shape[0]
    rows = total - halo
    acc = None
    for b in range(SUBLANES_F32):
        taps = [k for k in range(ksize) if (halo - (ksize - 1) + k) % SUBLANES_F32 == b]
        if not taps:
            continue
        shifted = win if b == 0 else pltpu.roll(win, total - b, axis=0)
        for k in taps:
            base = halo - (ksize - 1) + k - b
            w = _rows(w_ref[SUBLANES_F32 * k:SUBLANES_F32 * (k + 1), lanes], rows)
            term = shifted[base:base + rows, :] * w
            acc = term if acc is None else acc + term
    return acc


def _mod_kernel(c_ref, w_ref, b_ref, o_ref):
    o_ref[...] = _dot(jax.nn.silu(c_ref[...]), w_ref[...]) + b_ref[...]


def _adaln_mod(c, w_ada, b_ada):
    batch, d = c.shape
    n = w_ada.shape[1]
    bn = d
    return pl.pallas_call(
        _mod_kernel,
        grid=(n // bn,),
        in_specs=[
            pl.BlockSpec((batch, d), lambda j: (0, 0)),
            pl.BlockSpec((d, bn), lambda j: (0, j)),
            pl.BlockSpec((1, bn), lambda j: (0, j)),
        ],
        out_specs=pl.BlockSpec((batch, bn), lambda j: (0, j)),
        out_shape=jax.ShapeDtypeStruct((batch, n), jnp.float32),
        name="adaln_mod",
    )(c, w_ada, b_ada.reshape(1, n))


def _block_kernel(x_ref, mod_ref, w_in_ref, cw_ref, cb_ref, lng_ref, lnb_ref, scw_ref,
                  w_out_ref, w1_ref, w2_ref, gf_ref, o_ref,
                  h_buf, a_buf, q_buf, b_buf, mix_buf, x1_buf, u_buf, acc_buf):
    tt, d = h_buf.shape
    d_conf = a_buf.shape[1]
    d_sc = q_buf.shape[1]
    d_ff = w1_ref.shape[1]
    bf16 = jnp.bfloat16

    @pl.when(pl.program_id(1) == 0)
    def _():
        a_buf[0:A_HALO, :] = jnp.zeros((A_HALO, d_conf), jnp.float32)
        q_buf[0:Q_HALO, :] = jnp.zeros((Q_HALO, d_sc), jnp.float32)

    mod = mod_ref[0]
    sh1, sc1, g1 = mod[0:1], mod[1:2], mod[2:3]
    sh2, sc2, g2 = mod[3:4], mod[4:5], mod[5:6]

    x = x_ref[0]
    h_buf[...] = ((x * _rms_scale(x)) * (1.0 + sc1) + sh1).astype(bf16)
    h = h_buf[...]
    o0, o1, o2, o3 = d_conf, 2 * d_conf, 2 * d_conf + d_sc, 2 * d_conf + 2 * d_sc
    val = _dot(h, w_in_ref[:, 0:o0])
    gate = _dot(h, w_in_ref[:, o0:o1])
    a_buf[A_HALO:, :] = val * jax.nn.sigmoid(gate)
    b_buf[...] = _dot(h, w_in_ref[:, o1:o2])
    q_buf[Q_HALO:, :] = _dot(h, w_in_ref[:, o2:o3]) * _dot(h, w_in_ref[:, o3:])

    def conv_chunk(i, carry):
        r0 = pl.multiple_of(i * CONV_ROWS, CONV_ROWS)
        conv = []
        for lt in range(d_conf // LANES):
            lanes = slice(lt * LANES, (lt + 1) * LANES)
            win = a_buf[pl.ds(r0, CONV_ROWS + A_HALO), lanes]
            acc = _causal_taps(win, cw_ref, lanes, A_HALO, CONF_KERNEL)
            conv.append(acc + _rows(cb_ref[:, lanes], CONV_ROWS))
        acc = jnp.concatenate(conv, axis=-1)
        mu = jnp.mean(acc, axis=-1, keepdims=True)
        dev = acc - mu
        var = jnp.mean(dev * dev, axis=-1, keepdims=True)
        y = dev * lax.rsqrt(var + EPS) * _rows(lng_ref[...], CONV_ROWS) + _rows(lnb_ref[...], CONV_ROWS)
        mix_buf[pl.ds(r0, CONV_ROWS), 0:d_conf] = (y * jax.nn.sigmoid(y)).astype(bf16)

        for lt in range(d_sc // LANES):
            lanes = slice(lt * LANES, (lt + 1) * LANES)
            win = q_buf[pl.ds(r0, CONV_ROWS + Q_HALO), lanes]
            sconv = _causal_taps(win, scw_ref, lanes, Q_HALO, SC_KERNEL)
            s = b_buf[pl.ds(r0, CONV_ROWS), lanes] * sconv
            mix_buf[pl.ds(r0, CONV_ROWS), d_conf + lt * LANES:d_conf + (lt + 1) * LANES] = s.astype(bf16)
        return carry

    lax.fori_loop(0, tt // CONV_ROWS, conv_chunk, 0)

    a_buf[0:A_HALO, :] = a_buf[tt:tt + A_HALO, :]
    q_buf[0:Q_HALO, :] = q_buf[tt:tt + Q_HALO, :]

    x1 = x_ref[0] + g1 * _dot(mix_buf[...], w_out_ref[...])
    x1_buf[...] = x1
    h_buf[...] = ((x1 * _rms_scale(x1)) * (1.0 + sc2) + sh2).astype(bf16)

    h2 = h_buf[...]
    for ci in range(d_ff // FF_CHUNK):
        lo, hi = ci * FF_CHUNK, (ci + 1) * FF_CHUNK
        z = _dot(h2, w1_ref[:, lo:hi])
        u_buf[...] = jnp.square(jnp.maximum(z, 0.0)).astype(bf16)
        p = _dot(u_buf[...], w2_ref[lo:hi, :])
        if ci == 0:
            acc_buf[...] = p
        else:
            acc_buf[...] += p

    x2 = x1_buf[...] + g2 * acc_buf[...]
    o_ref[0] = (x2 * _rms_scale(x2)) * gf_ref[...]


def _resident(shape):
    return pl.BlockSpec(shape, lambda b, t: (0,) * len(shape), pipeline_mode=pl.Buffered(1))


def _fused_block(x, mod, w_in, cw, cb, lng, lnb, scw, w_out, w1, w2, gf):
    batch, seq, d = x.shape
    d_conf = cw.shape[1]
    d_sc = scw.shape[1]
    tt = TIME_TILE
    assert seq % tt == 0 and tt % CONV_ROWS == 0 and w1.shape[1] % FF_CHUNK == 0
    bf16 = jnp.bfloat16
    return pl.pallas_call(
        _block_kernel,
        grid=(batch, seq // tt),
        in_specs=[
            pl.BlockSpec((1, tt, d), lambda b, t: (b, t, 0)),
            pl.BlockSpec((1, N_MOD, d), lambda b, t: (b, 0, 0)),
            _resident(w_in.shape),
            _resident(cw.shape),
            _resident(cb.shape),
            _resident(lng.shape),
            _resident(lnb.shape),
            _resident(scw.shape),
            _resident(w_out.shape),
            _resident(w1.shape),
            _resident(w2.shape),
            _resident(gf.shape),
        ],
        out_specs=pl.BlockSpec((1, tt, d), lambda b, t: (b, t, 0)),
        out_shape=jax.ShapeDtypeStruct(x.shape, x.dtype),
        scratch_shapes=[
            pltpu.VMEM((tt, d), bf16),
            pltpu.VMEM((A_HALO + tt, d_conf), jnp.float32),
            pltpu.VMEM((Q_HALO + tt, d_sc), jnp.float32),
            pltpu.VMEM((tt, d_sc), jnp.float32),
            pltpu.VMEM((tt, d), bf16),
            pltpu.VMEM((tt, d), jnp.float32),
            pltpu.VMEM((tt, FF_CHUNK), bf16),
            pltpu.VMEM((tt, d), jnp.float32),
        ],
        compiler_params=pltpu.CompilerParams(
            dimension_semantics=("arbitrary", "arbitrary"),
            vmem_limit_bytes=VMEM_LIMIT_BYTES,
        ),
        name="fused_block",
    )(x, mod, w_in, cw, cb, lng, lnb, scw, w_out, w1, w2, gf)


def kernel(x, c, w_ada, b_ada, w_in, conf_dw_w, conf_dw_b, conf_ln_g, conf_ln_b,
           sc_conv_w, w_out, w_mlp1, w_mlp2, g_final):
    assert w_ada.shape[0] == 1, "fused block implements the depth-1 model"
    batch, _, d = x.shape
    bf16 = jnp.bfloat16
    mod = _adaln_mod(c, w_ada[0], b_ada[0]).reshape(batch, N_MOD, d)
    return _fused_block(
        x, mod, w_in[0].astype(bf16),
        _sublane_replicate(conf_dw_w[0]), _sublane_replicate(conf_dw_b),
        _sublane_replicate(conf_ln_g), _sublane_replicate(conf_ln_b),
        _sublane_replicate(sc_conv_w[0]),
        w_out[0].astype(bf16), w_mlp1[0].astype(bf16), w_mlp2[0].astype(bf16),
        g_final[None])
```

```python
import jax
import jax.numpy as jnp
from jax import lax
from jax.experimental import pallas as pl
from jax.experimental.pallas import tpu as pltpu

EPS = 1e-6
N_MOD = 6
CONF_KERNEL = 31
SC_KERNEL = 3

SUBLANES_F32 = 8
LANES = 128
VMEM_LIMIT_BYTES = 56 * 1024 * 1024

TIME_TILE = 512
CONV_ROWS = 64
FF_CHUNK = 1024
A_HALO = -(-(CONF_KERNEL - 1) // SUBLANES_F32) * SUBLANES_F32
Q_HALO = -(-(SC_KERNEL - 1) // SUBLANES_F32) * SUBLANES_F32


def _dot(a, b):
    return jnp.dot(a, b, preferred_element_type=jnp.float32)


def _rms_scale(v):
    return lax.rsqrt(jnp.mean(v * v, axis=-1, keepdims=True) + EPS)


def _rows(tile, rows):
    n = tile.shape[-1]
    return jnp.broadcast_to(tile[None], (rows // SUBLANES_F32, SUBLANES_F32, n)).reshape(rows, n)


def _sublane_replicate(p):
    k, n = p.shape
    return jnp.broadcast_to(p[:, None, :], (k, SUBLANES_F32, n)).reshape(k * SUBLANES_F32, n)


def _lane_blocks(n):
    return [slice(lt * LANES, (lt + 1) * LANES) for lt in range(n // LANES)]


def _causal_taps(buf, lt, r0, rows, w_ref, halo, ksize):
    lanes = slice(lt * LANES, (lt + 1) * LANES)
    acc = None
    for k in range(ksize):
        win = buf[lt, pl.ds(r0 + (halo - (ksize - 1) + k), rows, stride=1), :]
        term = win * _rows(w_ref[SUBLANES_F32 * k:SUBLANES_F32 * (k + 1), lanes], rows)
        acc = term if acc is None else acc + term
    return acc


def _mod_kernel(c_ref, w_ref, b_ref, o_ref):
    o_ref[...] = _dot(jax.nn.silu(c_ref[...]), w_ref[...]) + b_ref[...]


def _adaln_mod(c, w_ada, b_ada):
    batch, d = c.shape
    n = w_ada.shape[1]
    bn = d
    return pl.pallas_call(
        _mod_kernel,
        grid=(n // bn,),
        in_specs=[
            pl.BlockSpec((batch, d), lambda j: (0, 0)),
            pl.BlockSpec((d, bn), lambda j: (0, j)),
            pl.BlockSpec((1, bn), lambda j: (0, j)),
        ],
        out_specs=pl.BlockSpec((batch, bn), lambda j: (0, j)),
        out_shape=jax.ShapeDtypeStruct((batch, n), jnp.float32),
        name="adaln_mod",
    )(c, w_ada, b_ada.reshape(1, n))


def _block_kernel(x_ref, mod_ref, w_in_ref, cw_ref, cb_ref, lng_ref, lnb_ref, scw_ref,
                  w_out_ref, w1_ref, w2_ref, gf_ref, o_ref,
                  h_buf, a_buf, q_buf, b_buf, c_buf, mix_buf, x1_buf, u_buf, acc_buf):
    tt, d = h_buf.shape
    d_conf = a_buf.shape[0] * LANES
    d_sc = q_buf.shape[0] * LANES
    d_ff = w1_ref.shape[1]
    bf16 = jnp.bfloat16

    @pl.when(pl.program_id(1) == 0)
    def _():
        a_buf[:, 0:A_HALO, :] = jnp.zeros((d_conf // LANES, A_HALO, LANES), jnp.float32)
        q_buf[:, 0:Q_HALO, :] = jnp.zeros((d_sc // LANES, Q_HALO, LANES), jnp.float32)

    mod = mod_ref[0]
    sh1, sc1, g1 = mod[0:1], mod[1:2], mod[2:3]
    sh2, sc2, g2 = mod[3:4], mod[4:5], mod[5:6]

    x = x_ref[0]
    h_buf[...] = ((x * _rms_scale(x)) * (1.0 + sc1) + sh1).astype(bf16)
    h = h_buf[...]
    o0, o1, o2, o3 = d_conf, 2 * d_conf, 2 * d_conf + d_sc, 2 * d_conf + 2 * d_sc
    val = _dot(h, w_in_ref[:, 0:o0])
    gate = _dot(h, w_in_ref[:, o0:o1])
    a = val * jax.nn.sigmoid(gate)
    for lt, lanes in enumerate(_lane_blocks(d_conf)):
        a_buf[lt, A_HALO:, :] = a[:, lanes]
    b_buf[...] = _dot(h, w_in_ref[:, o1:o2])
    q = _dot(h, w_in_ref[:, o2:o3]) * _dot(h, w_in_ref[:, o3:])
    for lt, lanes in enumerate(_lane_blocks(d_sc)):
        q_buf[lt, Q_HALO:, :] = q[:, lanes]

    def row0(i):
        return i * CONV_ROWS if isinstance(i, int) else pl.multiple_of(i * CONV_ROWS, CONV_ROWS)

    def conv_rows(i):
        r0 = row0(i)
        for lt, lanes in enumerate(_lane_blocks(d_conf)):
            taps = _causal_taps(a_buf, lt, r0, CONV_ROWS, cw_ref, A_HALO, CONF_KERNEL)
            c_buf[pl.ds(r0, CONV_ROWS), lanes] = taps + _rows(cb_ref[:, lanes], CONV_ROWS)
        for lt, lanes in enumerate(_lane_blocks(d_sc)):
            sconv = _causal_taps(q_buf, lt, r0, CONV_ROWS, scw_ref, Q_HALO, SC_KERNEL)
            s = b_buf[pl.ds(r0, CONV_ROWS), lanes] * sconv
            mix_buf[pl.ds(r0, CONV_ROWS), d_conf + lt * LANES:d_conf + (lt + 1) * LANES] = s.astype(bf16)

    def norm_rows(i):
        r0 = row0(i)
        acc = c_buf[pl.ds(r0, CONV_ROWS), :]
        mu = jnp.mean(acc, axis=-1, keepdims=True)
        dev = acc - mu
        var = jnp.mean(dev * dev, axis=-1, keepdims=True)
        y = dev * lax.rsqrt(var + EPS) * _rows(lng_ref[...], CONV_ROWS) + _rows(lnb_ref[...], CONV_ROWS)
        mix_buf[pl.ds(r0, CONV_ROWS), 0:d_conf] = (y * jax.nn.sigmoid(y)).astype(bf16)

    def conv_step(i, carry):
        norm_rows(i - 1)
        conv_rows(i)
        return carry

    n_chunks = tt // CONV_ROWS
    conv_rows(0)
    lax.fori_loop(1, n_chunks, conv_step, 0)
    norm_rows(n_chunks - 1)

    a_buf[:, 0:A_HALO, :] = a_buf[:, tt:tt + A_HALO, :]
    q_buf[:, 0:Q_HALO, :] = q_buf[:, tt:tt + Q_HALO, :]

    x1 = x_ref[0] + g1 * _dot(mix_buf[...], w_out_ref[...])
    x1_buf[...] = x1
    h_buf[...] = ((x1 * _rms_scale(x1)) * (1.0 + sc2) + sh2).astype(bf16)

    h2 = h_buf[...]
    for ci in range(d_ff // FF_CHUNK):
        lo, hi = ci * FF_CHUNK, (ci + 1) * FF_CHUNK
        z = _dot(h2, w1_ref[:, lo:hi])
        u_buf[...] = jnp.square(jnp.maximum(z, 0.0)).astype(bf16)
        p = _dot(u_buf[...], w2_ref[lo:hi, :])
        if ci == 0:
            acc_buf[...] = p
        else:
            acc_buf[...] += p

    x2 = x1_buf[...] + g2 * acc_buf[...]
    o_ref[0] = (x2 * _rms_scale(x2)) * gf_ref[...]


def _resident(shape):
    return pl.BlockSpec(shape, lambda b, t: (0,) * len(shape), pipeline_mode=pl.Buffered(1))


def _fused_block(x, mod, w_in, cw, cb, lng, lnb, scw, w_out, w1, w2, gf):
    batch, seq, d = x.shape
    d_conf = cw.shape[1]
    d_sc = scw.shape[1]
    tt = TIME_TILE
    assert seq % tt == 0 and tt % CONV_ROWS == 0 and w1.shape[1] % FF_CHUNK == 0
    assert d_conf % LANES == 0 and d_sc % LANES == 0
    bf16 = jnp.bfloat16
    return pl.pallas_call(
        _block_kernel,
        grid=(batch, seq // tt),
        in_specs=[
            pl.BlockSpec((1, tt, d), lambda b, t: (b, t, 0)),
            pl.BlockSpec((1, N_MOD, d), lambda b, t: (b, 0, 0)),
            _resident(w_in.shape),
            _resident(cw.shape),
            _resident(cb.shape),
            _resident(lng.shape),
            _resident(lnb.shape),
            _resident(scw.shape),
            _resident(w_out.shape),
            _resident(w1.shape),
            _resident(w2.shape),
            _resident(gf.shape),
        ],
        out_specs=pl.BlockSpec((1, tt, d), lambda b, t: (b, t, 0)),
        out_shape=jax.ShapeDtypeStruct(x.shape, x.dtype),
        scratch_shapes=[
            pltpu.VMEM((tt, d), bf16),
            pltpu.VMEM((d_conf // LANES, A_HALO + tt, LANES), jnp.float32),
            pltpu.VMEM((d_sc // LANES, Q_HALO + tt, LANES), jnp.float32),
            pltpu.VMEM((tt, d_sc), jnp.float32),
            pltpu.VMEM((tt, d_conf), jnp.float32),
            pltpu.VMEM((tt, d), bf16),
            pltpu.VMEM((tt, d), jnp.float32),
            pltpu.VMEM((tt, FF_CHUNK), bf16),
            pltpu.VMEM((tt, d), jnp.float32),
        ],
        compiler_params=pltpu.CompilerParams(
            dimension_semantics=("arbitrary", "arbitrary"),
            vmem_limit_bytes=VMEM_LIMIT_BYTES,
        ),
        name="fused_block",
    )(x, mod, w_in, cw, cb, lng, lnb, scw, w_out, w1, w2, gf)


def kernel(x, c, w_ada, b_ada, w_in, conf_dw_w, conf_dw_b, conf_ln_g, conf_ln_b,
           sc_conv_w, w_out, w_mlp1, w_mlp2, g_final):
    assert w_ada.shape[0] == 1, "fused block implements the depth-1 model"
    batch, _, d = x.shape
    bf16 = jnp.bfloat16
    mod = _adaln_mod(c, w_ada[0], b_ada[0]).reshape(batch, N_MOD, d)
    return _fused_block(
        x, mod, w_in[0].astype(bf16),
        _sublane_replicate(conf_dw_w[0]), _sublane_replicate(conf_dw_b),
        _sublane_replicate(conf_ln_g), _sublane_replicate(conf_ln_b),
        _sublane_replicate(sc_conv_w[0]),
        w_out[0].astype(bf16), w_mlp1[0].astype(bf16), w_mlp2[0].astype(bf16),
        g_final[None])
```

```python
import functools

import jax
import jax.numpy as jnp
from jax import lax
from jax.experimental import pallas as pl
from jax.experimental.pallas import tpu as pltpu

EPS = 1e-6
N_MOD = 6
CONF_KERNEL = 31
SC_KERNEL = 3

SUBLANES_F32 = 8
LANES = 128
VMEM_LIMIT_BYTES = 56 * 1024 * 1024

TIME_TILE = 512
CONV_ROWS = 64
N_CHUNKS = TIME_TILE // CONV_ROWS
A_HALO = -(-(CONF_KERNEL - 1) // SUBLANES_F32) * SUBLANES_F32
Q_HALO = -(-(SC_KERNEL - 1) // SUBLANES_F32) * SUBLANES_F32


def _dot(a, b):
    return jnp.dot(a, b, preferred_element_type=jnp.float32)


def _rms_scale(v):
    return lax.rsqrt(jnp.mean(v * v, axis=-1, keepdims=True) + EPS)


def _rows(tile, rows):
    n = tile.shape[-1]
    return jnp.broadcast_to(tile[None], (rows // SUBLANES_F32, SUBLANES_F32, n)).reshape(rows, n)


def _sublane_replicate(p):
    k, n = p.shape
    return jnp.broadcast_to(p[:, None, :], (k, SUBLANES_F32, n)).reshape(k * SUBLANES_F32, n)


def _lane_blocks(n):
    return [slice(lt * LANES, (lt + 1) * LANES) for lt in range(n // LANES)]


def _order_after(v):
    bits = pltpu.bitcast(v[0:SUBLANES_F32, :], jnp.uint32)
    sixteen = jnp.uint32(16)
    zero = lax.shift_right_logical(lax.shift_right_logical(bits, sixteen), sixteen)
    return pltpu.bitcast(zero, jnp.float32)


def _causal_taps(buf, lt, r0, rows, w_ref, halo, ksize, after=None):
    lanes = slice(lt * LANES, (lt + 1) * LANES)
    acc = None
    for k in range(ksize):
        win = buf[lt, pl.ds(r0 + (halo - (ksize - 1) + k), rows, stride=1), :]
        w = w_ref[SUBLANES_F32 * k:SUBLANES_F32 * (k + 1), lanes]
        if after is not None:
            w = w + after
        term = win * _rows(w, rows)
        acc = term if acc is None else acc + term
    return acc


def _mod_kernel(c_ref, w_ref, b_ref, o_ref):
    o_ref[...] = _dot(jax.nn.silu(c_ref[...]), w_ref[...]) + b_ref[...]


def _adaln_mod(c, w_ada, b_ada):
    batch, d = c.shape
    n = w_ada.shape[1]
    bn = d
    return pl.pallas_call(
        _mod_kernel,
        grid=(n // bn,),
        in_specs=[
            pl.BlockSpec((batch, d), lambda j: (0, 0)),
            pl.BlockSpec((d, bn), lambda j: (0, j)),
            pl.BlockSpec((1, bn), lambda j: (0, j)),
        ],
        out_specs=pl.BlockSpec((batch, bn), lambda j: (0, j)),
        out_shape=jax.ShapeDtypeStruct((batch, n), jnp.float32),
        name="adaln_mod",
    )(c, w_ada, b_ada.reshape(1, n))


def _block_kernel(tiles_per_seq,
                  x_ref, modp_ref, modq_ref, w_in_ref, cw_ref, cb_ref, lng_ref, lnb_ref, scw_ref,
                  w_out_ref, w1_ref, w2_ref, gf_ref, o_ref,
                  h_buf, a_buf, q_buf, b_buf, c_buf, mix_buf, x1_buf, h2_buf, u_buf):
    tt, d = h_buf.shape
    d_conf = a_buf.shape[0] * LANES
    d_sc = q_buf.shape[0] * LANES
    bf16 = jnp.bfloat16
    step = pl.program_id(0)
    slot_p = lax.rem(step, 2)
    slot_q = 1 - slot_p

    @pl.when(step == 0)
    def _():
        x1_buf[1] = jnp.zeros((tt, d), jnp.float32)
        h2_buf[1] = jnp.zeros((tt, d), bf16)

    @pl.when(lax.rem(step, tiles_per_seq) == 0)
    def _():
        a_buf[:, 0:A_HALO, :] = jnp.zeros((d_conf // LANES, A_HALO, LANES), jnp.float32)
        q_buf[:, 0:Q_HALO, :] = jnp.zeros((d_sc // LANES, Q_HALO, LANES), jnp.float32)

    modp = modp_ref[0]
    sh1, sc1, g1 = modp[0:1], modp[1:2], modp[2:3]
    sh2, sc2 = modp[3:4], modp[4:5]
    g2 = modq_ref[0][5:6]

    x = x_ref[0]
    h_buf[...] = ((x * _rms_scale(x)) * (1.0 + sc1) + sh1).astype(bf16)
    h = h_buf[...]
    o0, o1, o2, o3 = d_conf, 2 * d_conf, 2 * d_conf + d_sc, 2 * d_conf + 2 * d_sc
    val = _dot(h, w_in_ref[:, 0:o0])
    gate = _dot(h, w_in_ref[:, o0:o1])
    a = val * jax.nn.sigmoid(gate)
    for lt, lanes in enumerate(_lane_blocks(d_conf)):
        a_buf[lt, A_HALO:, :] = a[:, lanes]
    b_buf[...] = _dot(h, w_in_ref[:, o1:o2])
    q = _dot(h, w_in_ref[:, o2:o3]) * _dot(h, w_in_ref[:, o3:])
    for lt, lanes in enumerate(_lane_blocks(d_sc)):
        q_buf[lt, Q_HALO:, :] = q[:, lanes]
    ffc = w1_ref.shape[2]

    def mlp_hidden(i):
        z = _dot(h2_buf[slot_q], w1_ref[i])
        u_buf[:, i * ffc:(i + 1) * ffc] = jnp.square(jnp.maximum(z, 0.0)).astype(bf16)
        return _order_after(z[:, 0:LANES])

    def row0(i):
        return i * CONV_ROWS if isinstance(i, int) else pl.multiple_of(i * CONV_ROWS, CONV_ROWS)

    def conv_rows(i, after):
        r0 = row0(i)
        for lt, lanes in enumerate(_lane_blocks(d_conf)):
            taps = _causal_taps(a_buf, lt, r0, CONV_ROWS, cw_ref, A_HALO, CONF_KERNEL, after)
            c_buf[pl.ds(r0, CONV_ROWS), lanes] = taps + _rows(cb_ref[:, lanes], CONV_ROWS)
            after = _order_after(taps)
        for lt, lanes in enumerate(_lane_blocks(d_sc)):
            sconv = _causal_taps(q_buf, lt, r0, CONV_ROWS, scw_ref, Q_HALO, SC_KERNEL, after)
            s = b_buf[pl.ds(r0, CONV_ROWS), lanes] * sconv
            mix_buf[pl.ds(r0, CONV_ROWS), d_conf + lt * LANES:d_conf + (lt + 1) * LANES] = s.astype(bf16)
            after = _order_after(sconv)
        return after

    def norm_rows(i):
        r0 = row0(i)
        acc = c_buf[pl.ds(r0, CONV_ROWS), :]
        mu = jnp.mean(acc, axis=-1, keepdims=True)
        dev = acc - mu
        var = jnp.mean(dev * dev, axis=-1, keepdims=True)
        y = dev * lax.rsqrt(var + EPS) * _rows(lng_ref[...], CONV_ROWS) + _rows(lnb_ref[...], CONV_ROWS)
        mix_buf[pl.ds(r0, CONV_ROWS), 0:d_conf] = (y * jax.nn.sigmoid(y)).astype(bf16)

    after = None
    for i in range(N_CHUNKS):
        released = mlp_hidden(i)
        after = conv_rows(i, released if after is None else after + released)
        norm_rows(i)

    a_buf[:, 0:A_HALO, :] = a_buf[:, tt:tt + A_HALO, :]
    q_buf[:, 0:Q_HALO, :] = q_buf[:, tt:tt + Q_HALO, :]

    x2 = x1_buf[slot_q] + g2 * _dot(u_buf[...], w2_ref[...])
    o_ref[0] = (x2 * _rms_scale(x2)) * gf_ref[...]

    x1 = x_ref[0] + g1 * _dot(mix_buf[...], w_out_ref[...])
    x1_buf[slot_p] = x1
    h2_buf[slot_p] = ((x1 * _rms_scale(x1)) * (1.0 + sc2) + sh2).astype(bf16)


def _resident(shape):
    return pl.BlockSpec(shape, lambda s: (0,) * len(shape), pipeline_mode=pl.Buffered(1))


def _fused_block(x, mod, w_in, cw, cb, lng, lnb, scw, w_out, w1, w2, gf):
    batch, seq, d = x.shape
    d_conf = cw.shape[1]
    d_sc = scw.shape[1]
    d_ff = w1.shape[1]
    tt = TIME_TILE
    assert seq % tt == 0 and d_ff % N_CHUNKS == 0
    assert d_conf % LANES == 0 and d_sc % LANES == 0
    tiles_per_seq = seq // tt
    n_tiles = batch * tiles_per_seq
    bf16 = jnp.bfloat16
    ffc = d_ff // N_CHUNKS
    w1c = w1.reshape(d, N_CHUNKS, ffc).transpose(1, 0, 2)

    def tile_p(s):
        return jnp.minimum(s, n_tiles - 1)

    def tile_q(s):
        return jnp.maximum(s - 1, 0)

    out = pl.pallas_call(
        functools.partial(_block_kernel, tiles_per_seq),
        grid=(n_tiles + 1,),
        in_specs=[
            pl.BlockSpec((1, tt, d), lambda s: (tile_p(s), 0, 0)),
            pl.BlockSpec((1, N_MOD, d), lambda s: (tile_p(s) // tiles_per_seq, 0, 0)),
            pl.BlockSpec((1, N_MOD, d), lambda s: (tile_q(s) // tiles_per_seq, 0, 0)),
            _resident(w_in.shape),
            _resident(cw.shape),
            _resident(cb.shape),
            _resident(lng.shape),
            _resident(lnb.shape),
            _resident(scw.shape),
            _resident(w_out.shape),
            _resident(w1c.shape),
            _resident(w2.shape),
            _resident(gf.shape),
        ],
        out_specs=pl.BlockSpec((1, tt, d), lambda s: (tile_q(s), 0, 0)),
        out_shape=jax.ShapeDtypeStruct((n_tiles, tt, d), x.dtype),
        scratch_shapes=[
            pltpu.VMEM((tt, d), bf16),
            pltpu.VMEM((d_conf // LANES, A_HALO + tt, LANES), jnp.float32),
            pltpu.VMEM((d_sc // LANES, Q_HALO + tt, LANES), jnp.float32),
            pltpu.VMEM((tt, d_sc), jnp.float32),
            pltpu.VMEM((tt, d_conf), jnp.float32),
            pltpu.VMEM((tt, d), bf16),
            pltpu.VMEM((2, tt, d), jnp.float32),
            pltpu.VMEM((2, tt, d), bf16),
            pltpu.VMEM((tt, d_ff), bf16),
        ],
        compiler_params=pltpu.CompilerParams(
            dimension_semantics=("arbitrary",),
            vmem_limit_bytes=VMEM_LIMIT_BYTES,
        ),
        name="fused_block",
    )(x.reshape(n_tiles, tt, d), mod, mod, w_in, cw, cb, lng, lnb, scw, w_out, w1c, w2, gf)
    return out.reshape(batch, seq, d)


def kernel(x, c, w_ada, b_ada, w_in, conf_dw_w, conf_dw_b, conf_ln_g, conf_ln_b,
           sc_conv_w, w_out, w_mlp1, w_mlp2, g_final):
    assert w_ada.shape[0] == 1, "fused block implements the depth-1 model"
    batch, _, d = x.shape
    bf16 = jnp.bfloat16
    mod = _adaln_mod(c, w_ada[0], b_ada[0]).reshape(batch, N_MOD, d)
    return _fused_block(
        x, mod, w_in[0].astype(bf16),
        _sublane_replicate(conf_dw_w[0]), _sublane_replicate(conf_dw_b),
        _sublane_replicate(conf_ln_g), _sublane_replicate(conf_ln_b),
        _sublane_replicate(sc_conv_w[0]),
        w_out[0].astype(bf16), w_mlp1[0].astype(bf16), w_mlp2[0].astype(bf16),
        g_final[None])
```

```python
import functools

import jax
import jax.numpy as jnp
from jax import lax
from jax.experimental import pallas as pl
from jax.experimental.pallas import tpu as pltpu

EPS = 1e-6
N_MOD = 6
CONF_KERNEL = 31
SC_KERNEL = 3

SUBLANES_F32 = 8
LANES = 128
VMEM_LIMIT_BYTES = 56 * 1024 * 1024

TIME_TILE = 512
CONV_ROWS = 64
N_CHUNKS = TIME_TILE // CONV_ROWS
GATE_AHEAD = 2
A_HALO = -(-(CONF_KERNEL - 1) // SUBLANES_F32) * SUBLANES_F32
Q_HALO = -(-(SC_KERNEL - 1) // SUBLANES_F32) * SUBLANES_F32


def _dot(a, b):
    return jnp.dot(a, b, preferred_element_type=jnp.float32)


def _rms_scale(v):
    return lax.rsqrt(jnp.mean(v * v, axis=-1, keepdims=True) + EPS)


def _rows(tile, rows):
    n = tile.shape[-1]
    return jnp.broadcast_to(tile[None], (rows // SUBLANES_F32, SUBLANES_F32, n)).reshape(rows, n)


def _sublane_replicate(p):
    k, n = p.shape
    return jnp.broadcast_to(p[:, None, :], (k, SUBLANES_F32, n)).reshape(k * SUBLANES_F32, n)


def _lane_blocks(n):
    return [slice(lt * LANES, (lt + 1) * LANES) for lt in range(n // LANES)]


def _zero_bits_after(v):
    bits = pltpu.bitcast(v[0:SUBLANES_F32, 0:LANES], jnp.uint32)
    sixteen = jnp.uint32(16)
    return lax.shift_right_logical(lax.shift_right_logical(bits, sixteen), sixteen)


def _order_after(v):
    return pltpu.bitcast(_zero_bits_after(v), jnp.float32)


def _index_after(v):
    return pltpu.bitcast(_zero_bits_after(v), jnp.int32)[0, 0]


def _causal_taps(buf, lt, r0, rows, w_ref, halo, ksize, after=None):
    lanes = slice(lt * LANES, (lt + 1) * LANES)
    acc = None
    for k in range(ksize):
        win = buf[lt, pl.ds(r0 + (halo - (ksize - 1) + k), rows, stride=1), :]
        w = w_ref[SUBLANES_F32 * k:SUBLANES_F32 * (k + 1), lanes]
        if after is not None:
            w = w + after
        term = win * _rows(w, rows)
        acc = term if acc is None else acc + term
    return acc


def _mod_kernel(c_ref, w_ref, b_ref, o_ref):
    o_ref[...] = _dot(jax.nn.silu(c_ref[...]), w_ref[...]) + b_ref[...]


def _adaln_mod(c, w_ada, b_ada):
    batch, d = c.shape
    n = w_ada.shape[1]
    bn = d
    return pl.pallas_call(
        _mod_kernel,
        grid=(n // bn,),
        in_specs=[
            pl.BlockSpec((batch, d), lambda j: (0, 0)),
            pl.BlockSpec((d, bn), lambda j: (0, j)),
            pl.BlockSpec((1, bn), lambda j: (0, j)),
        ],
        out_specs=pl.BlockSpec((batch, bn), lambda j: (0, j)),
        out_shape=jax.ShapeDtypeStruct((batch, n), jnp.float32),
        name="adaln_mod",
    )(c, w_ada, b_ada.reshape(1, n))


def _block_kernel(tiles_per_seq,
                  x_ref, modp_ref, modq_ref, w_in_ref, cw_ref, cb_ref, lng_ref, lnb_ref, scw_ref,
                  w_out_ref, w1_ref, w2_ref, gf_ref, o_ref,
                  h_buf, a_buf, q_buf, b_buf, c_buf, mix_buf, x1_buf, h2_buf, u_buf):
    tt, d = h_buf.shape
    d_conf = a_buf.shape[0] * LANES
    d_sc = q_buf.shape[0] * LANES
    bf16 = jnp.bfloat16
    step = pl.program_id(0)
    slot_p = lax.rem(step, 2)
    slot_q = 1 - slot_p

    @pl.when(step == 0)
    def _():
        x1_buf[1] = jnp.zeros((tt, d), jnp.float32)
        h2_buf[1] = jnp.zeros((tt, d), bf16)

    @pl.when(lax.rem(step, tiles_per_seq) == 0)
    def _():
        a_buf[:, 0:A_HALO, :] = jnp.zeros((d_conf // LANES, A_HALO, LANES), jnp.float32)
        q_buf[:, 0:Q_HALO, :] = jnp.zeros((d_sc // LANES, Q_HALO, LANES), jnp.float32)

    modp = modp_ref[0]
    sh1, sc1, g1 = modp[0:1], modp[1:2], modp[2:3]
    sh2, sc2 = modp[3:4], modp[4:5]
    g2 = modq_ref[0][5:6]

    x = x_ref[0]
    h_buf[...] = ((x * _rms_scale(x)) * (1.0 + sc1) + sh1).astype(bf16)
    h = h_buf[...]
    o0, o1, o2, o3 = d_conf, 2 * d_conf, 2 * d_conf + d_sc, 2 * d_conf + 2 * d_sc
    val = _dot(h, w_in_ref[:, 0:o0])
    gate = _dot(h, w_in_ref[:, o0:o1])
    a = val * jax.nn.sigmoid(gate)
    for lt, lanes in enumerate(_lane_blocks(d_conf)):
        a_buf[lt, A_HALO:, :] = a[:, lanes]
    b_buf[...] = _dot(h, w_in_ref[:, o1:o2])
    q = _dot(h, w_in_ref[:, o2:o3]) * _dot(h, w_in_ref[:, o3:])
    for lt, lanes in enumerate(_lane_blocks(d_sc)):
        q_buf[lt, Q_HALO:, :] = q[:, lanes]

    ffc = w1_ref.shape[1] // N_CHUNKS

    def mlp_hidden(i, gate):
        z = _dot(h2_buf[slot_q + gate], w1_ref[:, i * ffc:(i + 1) * ffc])
        u_buf[:, i * ffc:(i + 1) * ffc] = jnp.square(jnp.maximum(z, 0.0)).astype(bf16)
        return z

    def row0(i):
        return i * CONV_ROWS if isinstance(i, int) else pl.multiple_of(i * CONV_ROWS, CONV_ROWS)

    def conv_rows(i, z, after):
        r0 = row0(i)
        rows_per_block = tt // (d_conf // LANES)
        for lt, lanes in enumerate(_lane_blocks(d_conf)):
            released = _order_after(z[lt * rows_per_block:, :])
            after = released if after is None else after + released
            taps = _causal_taps(a_buf, lt, r0, CONV_ROWS, cw_ref, A_HALO, CONF_KERNEL, after)
            c_buf[pl.ds(r0, CONV_ROWS), lanes] = taps + _rows(cb_ref[:, lanes], CONV_ROWS)
            after = _order_after(taps)
        gate = _index_after(taps)
        for lt, lanes in enumerate(_lane_blocks(d_sc)):
            sconv = _causal_taps(q_buf, lt, r0, CONV_ROWS, scw_ref, Q_HALO, SC_KERNEL, after)
            s = b_buf[pl.ds(r0, CONV_ROWS), lanes] * sconv
            mix_buf[pl.ds(r0, CONV_ROWS), d_conf + lt * LANES:d_conf + (lt + 1) * LANES] = s.astype(bf16)
            after = _order_after(sconv)
        return after, gate

    def norm_rows(i):
        r0 = row0(i)
        acc = c_buf[pl.ds(r0, CONV_ROWS), :]
        mu = jnp.mean(acc, axis=-1, keepdims=True)
        dev = acc - mu
        var = jnp.mean(dev * dev, axis=-1, keepdims=True)
        y = dev * lax.rsqrt(var + EPS) * _rows(lng_ref[...], CONV_ROWS) + _rows(lnb_ref[...], CONV_ROWS)
        mix_buf[pl.ds(r0, CONV_ROWS), 0:d_conf] = (y * jax.nn.sigmoid(y)).astype(bf16)

    after = None
    gates = {}
    for i in range(N_CHUNKS):
        z = mlp_hidden(i, gates.get(i, 0))
        after, gates[i + GATE_AHEAD] = conv_rows(i, z, after)
        norm_rows(i)

    a_buf[:, 0:A_HALO, :] = a_buf[:, tt:tt + A_HALO, :]
    q_buf[:, 0:Q_HALO, :] = q_buf[:, tt:tt + Q_HALO, :]

    x2 = x1_buf[slot_q] + g2 * _dot(u_buf[...], w2_ref[...])
    o_ref[0] = (x2 * _rms_scale(x2)) * gf_ref[...]

    x1 = x_ref[0] + g1 * _dot(mix_buf[...], w_out_ref[...])
    x1_buf[slot_p] = x1
    h2_buf[slot_p] = ((x1 * _rms_scale(x1)) * (1.0 + sc2) + sh2).astype(bf16)


def _resident(shape):
    return pl.BlockSpec(shape, lambda s: (0,) * len(shape), pipeline_mode=pl.Buffered(1))


def _fused_block(x, mod, w_in, cw, cb, lng, lnb, scw, w_out, w1, w2, gf):
    batch, seq, d = x.shape
    d_conf = cw.shape[1]
    d_sc = scw.shape[1]
    d_ff = w1.shape[1]
    tt = TIME_TILE
    assert seq % tt == 0 and d_ff % N_CHUNKS == 0
    assert d_conf % LANES == 0 and d_sc % LANES == 0
    tiles_per_seq = seq // tt
    n_tiles = batch * tiles_per_seq
    bf16 = jnp.bfloat16

    def tile_p(s):
        return jnp.minimum(s, n_tiles - 1)

    def tile_q(s):
        return jnp.maximum(s - 1, 0)

    out = pl.pallas_call(
        functools.partial(_block_kernel, tiles_per_seq),
        grid=(n_tiles + 1,),
        in_specs=[
            pl.BlockSpec((1, tt, d), lambda s: (tile_p(s), 0, 0)),
            pl.BlockSpec((1, N_MOD, d), lambda s: (tile_p(s) // tiles_per_seq, 0, 0)),
            pl.BlockSpec((1, N_MOD, d), lambda s: (tile_q(s) // tiles_per_seq, 0, 0)),
            _resident(w_in.shape),
            _resident(cw.shape),
            _resident(cb.shape),
            _resident(lng.shape),
            _resident(lnb.shape),
            _resident(scw.shape),
            _resident(w_out.shape),
            _resident(w1.shape),
            _resident(w2.shape),
            _resident(gf.shape),
        ],
        out_specs=pl.BlockSpec((1, tt, d), lambda s: (tile_q(s), 0, 0)),
        out_shape=jax.ShapeDtypeStruct((n_tiles, tt, d), x.dtype),
        scratch_shapes=[
            pltpu.VMEM((tt, d), bf16),
            pltpu.VMEM((d_conf // LANES, A_HALO + tt, LANES), jnp.float32),
            pltpu.VMEM((d_sc // LANES, Q_HALO + tt, LANES), jnp.float32),
            pltpu.VMEM((tt, d_sc), jnp.float32),
            pltpu.VMEM((tt, d_conf), jnp.float32),
            pltpu.VMEM((tt, d), bf16),
            pltpu.VMEM((2, tt, d), jnp.float32),
            pltpu.VMEM((2, tt, d), bf16),
            pltpu.VMEM((tt, d_ff), bf16),
        ],
        compiler_params=pltpu.CompilerParams(
            dimension_semantics=("arbitrary",),
            vmem_limit_bytes=VMEM_LIMIT_BYTES,
        ),
        name="fused_block",
    )(x.reshape(n_tiles, tt, d), mod, mod, w_in, cw, cb, lng, lnb, scw, w_out, w1, w2, gf)
    return out.reshape(batch, seq, d)


def kernel(x, c, w_ada, b_ada, w_in, conf_dw_w, conf_dw_b, conf_ln_g, conf_ln_b,
           sc_conv_w, w_out, w_mlp1, w_mlp2, g_final):
    assert w_ada.shape[0] == 1, "fused block implements the depth-1 model"
    batch, _, d = x.shape
    bf16 = jnp.bfloat16
    mod = _adaln_mod(c, w_ada[0], b_ada[0]).reshape(batch, N_MOD, d)
    return _fused_block(
        x, mod, w_in[0].astype(bf16),
        _sublane_replicate(conf_dw_w[0]), _sublane_replicate(conf_dw_b),
        _sublane_replicate(conf_ln_g), _sublane_replicate(conf_ln_b),
        _sublane_replicate(sc_conv_w[0]),
        w_out[0].astype(bf16), w_mlp1[0].astype(bf16), w_mlp2[0].astype(bf16),
        g_final[None])
```

```python
import functools

import jax
import jax.numpy as jnp
from jax import lax
from jax.experimental import pallas as pl
from jax.experimental.pallas import tpu as pltpu

EPS = 1e-6
N_MOD = 6
CONF_KERNEL = 31
SC_KERNEL = 3

SUBLANES_F32 = 8
LANES = 128
VMEM_LIMIT_BYTES = 56 * 1024 * 1024

TIME_TILE = 512
CONV_ROWS = 64
N_CHUNKS = TIME_TILE // CONV_ROWS
HIDDEN_GROUP = 2
AFTER_EVERY = 2
GATE_AHEAD = 2
GATE_BLOCK = 3
A_HALO = -(-(CONF_KERNEL - 1) // SUBLANES_F32) * SUBLANES_F32
Q_HALO = -(-(SC_KERNEL - 1) // SUBLANES_F32) * SUBLANES_F32


def _dot(a, b):
    return jnp.dot(a, b, preferred_element_type=jnp.float32)


def _rms_scale(v):
    return lax.rsqrt(jnp.mean(v * v, axis=-1, keepdims=True) + EPS)


def _rows(tile, rows):
    n = tile.shape[-1]
    return jnp.broadcast_to(tile[None], (rows // SUBLANES_F32, SUBLANES_F32, n)).reshape(rows, n)


def _sublane_replicate(p):
    k, n = p.shape
    return jnp.broadcast_to(p[:, None, :], (k, SUBLANES_F32, n)).reshape(k * SUBLANES_F32, n)


def _pair_columns(wa, wb):
    k, n = wa.shape
    blocks = [w.reshape(k, n // LANES, 1, LANES) for w in (wa, wb)]
    return jnp.concatenate(blocks, axis=2).reshape(k, 2 * n)


def _lane_blocks(n):
    return [slice(lt * LANES, (lt + 1) * LANES) for lt in range(n // LANES)]


def _zero_bits_after(v):
    bits = pltpu.bitcast(v[0:SUBLANES_F32, 0:LANES], jnp.uint32)
    sixteen = jnp.uint32(16)
    return lax.shift_right_logical(lax.shift_right_logical(bits, sixteen), sixteen)


def _order_after(v):
    return pltpu.bitcast(_zero_bits_after(v), jnp.float32)


def _index_after(v):
    return pltpu.bitcast(_zero_bits_after(v), jnp.int32)[0, 0]


def _causal_taps(buf, lt, r0, rows, w_ref, halo, ksize, after=None):
    lanes = slice(lt * LANES, (lt + 1) * LANES)
    acc = None
    for k in range(ksize):
        win = buf[lt, pl.ds(r0 + (halo - (ksize - 1) + k), rows, stride=1), :]
        w = w_ref[SUBLANES_F32 * k:SUBLANES_F32 * (k + 1), lanes]
        if after is not None and k % AFTER_EVERY == 0:
            w = w + after
        term = win * _rows(w, rows)
        acc = term if acc is None else acc + term
    return acc


def _mod_kernel(c_ref, w_ref, b_ref, o_ref):
    o_ref[...] = _dot(jax.nn.silu(c_ref[...]), w_ref[...]) + b_ref[...]


def _adaln_mod(c, w_ada, b_ada):
    batch, d = c.shape
    n = w_ada.shape[1]
    bn = d
    return pl.pallas_call(
        _mod_kernel,
        grid=(n // bn,),
        in_specs=[
            pl.BlockSpec((batch, d), lambda j: (0, 0)),
            pl.BlockSpec((d, bn), lambda j: (0, j)),
            pl.BlockSpec((1, bn), lambda j: (0, j)),
        ],
        out_specs=pl.BlockSpec((batch, bn), lambda j: (0, j)),
        out_shape=jax.ShapeDtypeStruct((batch, n), jnp.float32),
        name="adaln_mod",
    )(c, w_ada, b_ada.reshape(1, n))


def _block_kernel(tiles_per_seq,
                  x_ref, modp_ref, modq_ref, w_in_ref, cw_ref, cb_ref, lng_ref, lnb_ref, scw_ref,
                  w_out_ref, w1_ref, w2_ref, gf_ref, o_ref,
                  h_buf, a_buf, q_buf, b_buf, c_buf, mix_buf, x1_buf, h2_buf, u_buf):
    tt, d = h_buf.shape
    d_conf = a_buf.shape[0] * LANES
    d_sc = q_buf.shape[0] * LANES
    bf16 = jnp.bfloat16
    step = pl.program_id(0)
    slot_p = lax.rem(step, 2)
    slot_q = 1 - slot_p

    @pl.when(step == 0)
    def _():
        x1_buf[1] = jnp.zeros((tt, d), jnp.float32)
        h2_buf[1] = jnp.zeros((tt, d), bf16)

    @pl.when(lax.rem(step, tiles_per_seq) == 0)
    def _():
        a_buf[:, 0:A_HALO, :] = jnp.zeros((d_conf // LANES, A_HALO, LANES), jnp.float32)
        q_buf[:, 0:Q_HALO, :] = jnp.zeros((d_sc // LANES, Q_HALO, LANES), jnp.float32)

    modp = modp_ref[0]
    sh1, sc1, g1 = modp[0:1], modp[1:2], modp[2:3]
    sh2, sc2 = modp[3:4], modp[4:5]
    g2 = modq_ref[0][5:6]

    x = x_ref[0]
    h_buf[...] = ((x * _rms_scale(x)) * (1.0 + sc1) + sh1).astype(bf16)
    proj = _dot(h_buf[...], w_in_ref[...])
    for lt in range(d_conf // LANES):
        lo = 2 * lt * LANES
        a_buf[lt, A_HALO:, :] = (proj[:, lo:lo + LANES]
                                 * jax.nn.sigmoid(proj[:, lo + LANES:lo + 2 * LANES]))
    for lt in range(d_sc // LANES):
        lo = 2 * d_conf + 2 * lt * LANES
        q_buf[lt, Q_HALO:, :] = proj[:, lo:lo + LANES] * proj[:, lo + LANES:lo + 2 * LANES]
    b_buf[...] = proj[:, 2 * d_conf + 2 * d_sc:]

    ffc = w1_ref.shape[1] // N_CHUNKS

    def mlp_hidden(i, n, gate):
        lo, hi = i * ffc, (i + n) * ffc
        z = _dot(h2_buf[slot_q + gate], w1_ref[:, lo:hi])
        u_buf[:, lo:hi] = jnp.square(jnp.maximum(z, 0.0)).astype(bf16)
        return [z[:, c * ffc:(c + 1) * ffc] for c in range(n)]

    def row0(i):
        return i * CONV_ROWS if isinstance(i, int) else pl.multiple_of(i * CONV_ROWS, CONV_ROWS)

    def conv_rows(i, z, after):
        r0 = row0(i)
        rows_per_block = tt // (d_conf // LANES)
        for lt, lanes in enumerate(_lane_blocks(d_conf)):
            released = _order_after(z[lt * rows_per_block:, :])
            after = released if after is None else after + released
            taps = _causal_taps(a_buf, lt, r0, CONV_ROWS, cw_ref, A_HALO, CONF_KERNEL, after)
            c_buf[pl.ds(r0, CONV_ROWS), lanes] = taps + _rows(cb_ref[:, lanes], CONV_ROWS)
            after = _order_after(taps)
            if lt == GATE_BLOCK:
                gate = _index_after(taps)
        for lt, lanes in enumerate(_lane_blocks(d_sc)):
            sconv = _causal_taps(q_buf, lt, r0, CONV_ROWS, scw_ref, Q_HALO, SC_KERNEL, after)
            s = b_buf[pl.ds(r0, CONV_ROWS), lanes] * sconv
            mix_buf[pl.ds(r0, CONV_ROWS), d_conf + lt * LANES:d_conf + (lt + 1) * LANES] = s.astype(bf16)
            after = _order_after(sconv)
        return after, gate

    def norm_rows(i):
        r0 = row0(i)
        acc = c_buf[pl.ds(r0, CONV_ROWS), :]
        mu = jnp.mean(acc, axis=-1, keepdims=True)
        dev = acc - mu
        var = jnp.mean(dev * dev, axis=-1, keepdims=True)
        y = dev * lax.rsqrt(var + EPS) * _rows(lng_ref[...], CONV_ROWS) + _rows(lnb_ref[...], CONV_ROWS)
        mix_buf[pl.ds(r0, CONV_ROWS), 0:d_conf] = (y * jax.nn.sigmoid(y)).astype(bf16)

    after = None
    gates = {}
    for i0 in range(0, N_CHUNKS, HIDDEN_GROUP):
        zs = mlp_hidden(i0, HIDDEN_GROUP, gates.get(i0, 0))
        for c, z in enumerate(zs):
            i = i0 + c
            after, gates[i + GATE_AHEAD] = conv_rows(i, z, after)
            norm_rows(i)

    a_buf[:, 0:A_HALO, :] = a_buf[:, tt:tt + A_HALO, :]
    q_buf[:, 0:Q_HALO, :] = q_buf[:, tt:tt + Q_HALO, :]

    x2 = x1_buf[slot_q] + g2 * _dot(u_buf[...], w2_ref[...])
    o_ref[0] = (x2 * _rms_scale(x2)) * gf_ref[...]

    x1 = x_ref[0] + g1 * _dot(mix_buf[...], w_out_ref[...])
    x1_buf[slot_p] = x1
    h2_buf[slot_p] = ((x1 * _rms_scale(x1)) * (1.0 + sc2) + sh2).astype(bf16)


def _resident(shape):
    return pl.BlockSpec(shape, lambda s: (0,) * len(shape), pipeline_mode=pl.Buffered(1))


def _fused_block(x, mod, w_in, cw, cb, lng, lnb, scw, w_out, w1, w2, gf):
    batch, seq, d = x.shape
    d_conf = cw.shape[1]
    d_sc = scw.shape[1]
    d_ff = w1.shape[1]
    tt = TIME_TILE
    assert seq % tt == 0 and d_ff % N_CHUNKS == 0
    assert d_conf % LANES == 0 and d_sc % LANES == 0
    tiles_per_seq = seq // tt
    n_tiles = batch * tiles_per_seq
    bf16 = jnp.bfloat16

    def tile_p(s):
        return jnp.minimum(s, n_tiles - 1)

    def tile_q(s):
        return jnp.maximum(s - 1, 0)

    out = pl.pallas_call(
        functools.partial(_block_kernel, tiles_per_seq),
        grid=(n_tiles + 1,),
        in_specs=[
            pl.BlockSpec((1, tt, d), lambda s: (tile_p(s), 0, 0)),
            pl.BlockSpec((1, N_MOD, d), lambda s: (tile_p(s) // tiles_per_seq, 0, 0)),
            pl.BlockSpec((1, N_MOD, d), lambda s: (tile_q(s) // tiles_per_seq, 0, 0)),
            _resident(w_in.shape),
            _resident(cw.shape),
            _resident(cb.shape),
            _resident(lng.shape),
            _resident(lnb.shape),
            _resident(scw.shape),
            _resident(w_out.shape),
            _resident(w1.shape),
            _resident(w2.shape),
            _resident(gf.shape),
        ],
        out_specs=pl.BlockSpec((1, tt, d), lambda s: (tile_q(s), 0, 0)),
        out_shape=jax.ShapeDtypeStruct((n_tiles, tt, d), x.dtype),
        scratch_shapes=[
            pltpu.VMEM((tt, d), bf16),
            pltpu.VMEM((d_conf // LANES, A_HALO + tt, LANES), jnp.float32),
            pltpu.VMEM((d_sc // LANES, Q_HALO + tt, LANES), jnp.float32),
            pltpu.VMEM((tt, d_sc), jnp.float32),
            pltpu.VMEM((tt, d_conf), jnp.float32),
            pltpu.VMEM((tt, d), bf16),
            pltpu.VMEM((2, tt, d), jnp.float32),
            pltpu.VMEM((2, tt, d), bf16),
            pltpu.VMEM((tt, d_ff), bf16),
        ],
        compiler_params=pltpu.CompilerParams(
            dimension_semantics=("arbitrary",),
            vmem_limit_bytes=VMEM_LIMIT_BYTES,
        ),
        name="fused_block",
    )(x.reshape(n_tiles, tt, d), mod, mod, w_in, cw, cb, lng, lnb, scw, w_out, w1, w2, gf)
    return out.reshape(batch, seq, d)


def kernel(x, c, w_ada, b_ada, w_in, conf_dw_w, conf_dw_b, conf_ln_g, conf_ln_b,
           sc_conv_w, w_out, w_mlp1, w_mlp2, g_final):
    assert w_ada.shape[0] == 1, "fused block implements the depth-1 model"
    batch, _, d = x.shape
    bf16 = jnp.bfloat16
    mod = _adaln_mod(c, w_ada[0], b_ada[0]).reshape(batch, N_MOD, d)
    d_conf, d_sc = conf_dw_w.shape[2], sc_conv_w.shape[2]
    w_val, w_gate, w_b, w_c, w_h = jnp.split(
        w_in[0].astype(bf16), [d_conf, 2 * d_conf, 2 * d_conf + d_sc, 2 * d_conf + 2 * d_sc], axis=1)
    w_in_paired = jnp.concatenate([_pair_columns(w_val, w_gate), _pair_columns(w_c, w_h), w_b], axis=1)
    return _fused_block(
        x, mod, w_in_paired,
        _sublane_replicate(conf_dw_w[0]), _sublane_replicate(conf_dw_b),
        _sublane_replicate(conf_ln_g), _sublane_replicate(conf_ln_b),
        _sublane_replicate(sc_conv_w[0]),
        w_out[0].astype(bf16), w_mlp1[0].astype(bf16), w_mlp2[0].astype(bf16),
        g_final[None])
```

```python
import functools

import jax
import jax.numpy as jnp
from jax import lax
from jax.experimental import pallas as pl
from jax.experimental.pallas import tpu as pltpu

EPS = 1e-6
N_MOD = 6
CONF_KERNEL = 31
SC_KERNEL = 3

SUBLANES_F32 = 8
LANES = 128
VMEM_LIMIT_BYTES = 56 * 1024 * 1024

TIME_TILE = 512
CONV_ROWS = 64
N_CHUNKS = TIME_TILE // CONV_ROWS
HIDDEN_GROUP = 2
AFTER_EVERY = 2
GATE_AHEAD = 2
GATE_BLOCK = 3
A_HALO = -(-(CONF_KERNEL - 1) // SUBLANES_F32) * SUBLANES_F32
Q_HALO = -(-(SC_KERNEL - 1) // SUBLANES_F32) * SUBLANES_F32


def _dot(a, b):
    return jnp.dot(a, b, preferred_element_type=jnp.float32)


def _rms_scale(v):
    return lax.rsqrt(jnp.mean(v * v, axis=-1, keepdims=True) + EPS)


def _rows(tile, rows):
    n = tile.shape[-1]
    return jnp.broadcast_to(tile[None], (rows // SUBLANES_F32, SUBLANES_F32, n)).reshape(rows, n)


def _sublane_replicate(p):
    k, n = p.shape
    return jnp.broadcast_to(p[:, None, :], (k, SUBLANES_F32, n)).reshape(k * SUBLANES_F32, n)


def _pair_columns(wa, wb):
    k, n = wa.shape
    blocks = [w.reshape(k, n // LANES, 1, LANES) for w in (wa, wb)]
    return jnp.concatenate(blocks, axis=2).reshape(k, 2 * n)


def _lane_blocks(n):
    return [slice(lt * LANES, (lt + 1) * LANES) for lt in range(n // LANES)]


def _zero_bits_after(v):
    bits = pltpu.bitcast(v[0:SUBLANES_F32, 0:LANES], jnp.uint32)
    sixteen = jnp.uint32(16)
    return lax.shift_right_logical(lax.shift_right_logical(bits, sixteen), sixteen)


def _order_after(v):
    return pltpu.bitcast(_zero_bits_after(v), jnp.float32)


def _index_after(v):
    return pltpu.bitcast(_zero_bits_after(v), jnp.int32)[0, 0]


def _causal_taps(buf, lt, r0, rows, w_ref, halo, ksize, after=None):
    lanes = slice(lt * LANES, (lt + 1) * LANES)
    acc = None
    for k in range(ksize):
        win = buf[lt, pl.ds(r0 + (halo - (ksize - 1) + k), rows, stride=1), :]
        w = w_ref[SUBLANES_F32 * k:SUBLANES_F32 * (k + 1), lanes]
        if after is not None and k % AFTER_EVERY == 0:
            w = w + after
        term = win * _rows(w, rows)
        acc = term if acc is None else acc + term
    return acc


def _mod_kernel(c_ref, w_ref, b_ref, o_ref):
    o_ref[...] = _dot(jax.nn.silu(c_ref[...]), w_ref[...]) + b_ref[...]


def _adaln_mod(c, w_ada, b_ada):
    batch, d = c.shape
    n = w_ada.shape[1]
    bn = d
    return pl.pallas_call(
        _mod_kernel,
        grid=(n // bn,),
        in_specs=[
            pl.BlockSpec((batch, d), lambda j: (0, 0)),
            pl.BlockSpec((d, bn), lambda j: (0, j)),
            pl.BlockSpec((1, bn), lambda j: (0, j)),
        ],
        out_specs=pl.BlockSpec((batch, bn), lambda j: (0, j)),
        out_shape=jax.ShapeDtypeStruct((batch, n), jnp.float32),
        name="adaln_mod",
    )(c, w_ada, b_ada.reshape(1, n))


def _block_kernel(tiles_per_seq,
                  x_ref, modp_ref, modq_ref, w_in_ref, cw_ref, cb_ref, lng_ref, lnb_ref, scw_ref,
                  w_out_ref, w1_ref, w2_ref, gf_ref, o_ref,
                  h_buf, a_buf, q_buf, b_buf, c_buf, mix_buf, x1_buf, h2_buf, u_buf):
    tt, d = h_buf.shape
    d_conf = a_buf.shape[0] * LANES
    d_sc = q_buf.shape[0] * LANES
    bf16 = jnp.bfloat16
    step = pl.program_id(0)
    slot_p = lax.rem(step, 2)
    slot_q = 1 - slot_p

    @pl.when(step == 0)
    def _():
        x1_buf[1] = jnp.zeros((tt, d), jnp.float32)
        h2_buf[1] = jnp.zeros((tt, d), bf16)

    @pl.when(lax.rem(step, tiles_per_seq) == 0)
    def _():
        a_buf[:, 0:A_HALO, :] = jnp.zeros((d_conf // LANES, A_HALO, LANES), jnp.float32)
        q_buf[:, 0:Q_HALO, :] = jnp.zeros((d_sc // LANES, Q_HALO, LANES), jnp.float32)

    modp = modp_ref[0]
    sh1, sc1, g1 = modp[0:1], modp[1:2], modp[2:3]
    sh2, sc2 = modp[3:4], modp[4:5]
    g2 = modq_ref[0][5:6]

    ffc = w1_ref.shape[1] // N_CHUNKS

    def mlp_hidden(i, n, gate):
        lo, hi = i * ffc, (i + n) * ffc
        z = _dot(h2_buf[slot_q + gate], w1_ref[:, lo:hi])
        u_buf[:, lo:hi] = jnp.square(jnp.maximum(z, 0.0)).astype(bf16)
        return [z[:, c * ffc:(c + 1) * ffc] for c in range(n)]

    x = x_ref[0]
    h_buf[...] = ((x * _rms_scale(x)) * (1.0 + sc1) + sh1).astype(bf16)
    proj = _dot(h_buf[...], w_in_ref[...])
    for lt in range(d_conf // LANES):
        lo = 2 * lt * LANES
        a_buf[lt, A_HALO:, :] = (proj[:, lo:lo + LANES]
                                 * jax.nn.sigmoid(proj[:, lo + LANES:lo + 2 * LANES]))
    for lt in range(d_sc // LANES):
        lo = 2 * d_conf + 2 * lt * LANES
        q_buf[lt, Q_HALO:, :] = proj[:, lo:lo + LANES] * proj[:, lo + LANES:lo + 2 * LANES]
    b_buf[...] = proj[:, 2 * d_conf + 2 * d_sc:]

    def row0(i):
        return i * CONV_ROWS if isinstance(i, int) else pl.multiple_of(i * CONV_ROWS, CONV_ROWS)

    def conv_rows(i, z, after):
        r0 = row0(i)
        rows_per_block = tt // (d_conf // LANES)
        for lt, lanes in enumerate(_lane_blocks(d_conf)):
            released = _order_after(z[lt * rows_per_block:, :])
            after = released if after is None else after + released
            taps = _causal_taps(a_buf, lt, r0, CONV_ROWS, cw_ref, A_HALO, CONF_KERNEL, after)
            c_buf[0, pl.ds(r0, CONV_ROWS), lanes] = taps + _rows(cb_ref[:, lanes], CONV_ROWS)
            after = _order_after(taps)
            if lt == GATE_BLOCK:
                gate = _index_after(taps)
        for lt, lanes in enumerate(_lane_blocks(d_sc)):
            sconv = _causal_taps(q_buf, lt, r0, CONV_ROWS, scw_ref, Q_HALO, SC_KERNEL, after)
            s = b_buf[pl.ds(r0, CONV_ROWS), lanes] * sconv
            mix_buf[pl.ds(r0, CONV_ROWS), d_conf + lt * LANES:d_conf + (lt + 1) * LANES] = s.astype(bf16)
            after = _order_after(sconv)
        return after, gate

    def norm_rows(i, gate):
        r0 = row0(i)
        acc = c_buf[gate, pl.ds(r0, CONV_ROWS), :]
        mu = jnp.mean(acc, axis=-1, keepdims=True)
        dev = acc - mu
        var = jnp.mean(dev * dev, axis=-1, keepdims=True)
        y = dev * lax.rsqrt(var + EPS) * _rows(lng_ref[...], CONV_ROWS) + _rows(lnb_ref[...], CONV_ROWS)
        mix_buf[pl.ds(r0, CONV_ROWS), 0:d_conf] = (y * jax.nn.sigmoid(y)).astype(bf16)

    after = None
    gates = {}
    for i0 in range(0, N_CHUNKS, HIDDEN_GROUP):
        zs = mlp_hidden(i0, HIDDEN_GROUP, gates.get(i0, 0))
        for c, z in enumerate(zs):
            i = i0 + c
            after, gates[i + GATE_AHEAD] = conv_rows(i, z, after)

    a_buf[:, 0:A_HALO, :] = a_buf[:, tt:tt + A_HALO, :]
    q_buf[:, 0:Q_HALO, :] = q_buf[:, tt:tt + Q_HALO, :]

    mlp = _dot(u_buf[...], w2_ref[...])
    x2 = x1_buf[slot_q] + g2 * mlp
    o_ref[0] = (x2 * _rms_scale(x2)) * gf_ref[...]

    mlp_started = _index_after(mlp)
    for i in range(N_CHUNKS):
        norm_rows(i, mlp_started)

    x1 = x_ref[0] + g1 * _dot(mix_buf[...], w_out_ref[...])
    x1_buf[slot_p] = x1
    h2_buf[slot_p] = ((x1 * _rms_scale(x1)) * (1.0 + sc2) + sh2).astype(bf16)


def _resident(shape):
    return pl.BlockSpec(shape, lambda s: (0,) * len(shape), pipeline_mode=pl.Buffered(1))


def _fused_block(x, mod, w_in, cw, cb, lng, lnb, scw, w_out, w1, w2, gf):
    batch, seq, d = x.shape
    d_conf = cw.shape[1]
    d_sc = scw.shape[1]
    d_ff = w1.shape[1]
    tt = TIME_TILE
    assert seq % tt == 0 and d_ff % N_CHUNKS == 0 and N_CHUNKS % HIDDEN_GROUP == 0
    assert d_conf % LANES == 0 and d_sc % LANES == 0
    tiles_per_seq = seq // tt
    n_tiles = batch * tiles_per_seq
    bf16 = jnp.bfloat16

    def tile_p(s):
        return jnp.minimum(s, n_tiles - 1)

    def tile_q(s):
        return jnp.maximum(s - 1, 0)

    out = pl.pallas_call(
        functools.partial(_block_kernel, tiles_per_seq),
        grid=(n_tiles + 1,),
        in_specs=[
            pl.BlockSpec((1, tt, d), lambda s: (tile_p(s), 0, 0)),
            pl.BlockSpec((1, N_MOD, d), lambda s: (tile_p(s) // tiles_per_seq, 0, 0)),
            pl.BlockSpec((1, N_MOD, d), lambda s: (tile_q(s) // tiles_per_seq, 0, 0)),
            _resident(w_in.shape),
            _resident(cw.shape),
            _resident(cb.shape),
            _resident(lng.shape),
            _resident(lnb.shape),
            _resident(scw.shape),
            _resident(w_out.shape),
            _resident(w1.shape),
            _resident(w2.shape),
            _resident(gf.shape),
        ],
        out_specs=pl.BlockSpec((1, tt, d), lambda s: (tile_q(s), 0, 0)),
        out_shape=jax.ShapeDtypeStruct((n_tiles, tt, d), x.dtype),
        scratch_shapes=[
            pltpu.VMEM((tt, d), bf16),
            pltpu.VMEM((d_conf // LANES, A_HALO + tt, LANES), jnp.float32),
            pltpu.VMEM((d_sc // LANES, Q_HALO + tt, LANES), jnp.float32),
            pltpu.VMEM((tt, d_sc), jnp.float32),
            pltpu.VMEM((1, tt, d_conf), jnp.float32),
            pltpu.VMEM((tt, d), bf16),
            pltpu.VMEM((2, tt, d), jnp.float32),
            pltpu.VMEM((2, tt, d), bf16),
            pltpu.VMEM((tt, d_ff), bf16),
        ],
        compiler_params=pltpu.CompilerParams(
            dimension_semantics=("arbitrary",),
            vmem_limit_bytes=VMEM_LIMIT_BYTES,
        ),
        name="fused_block",
    )(x.reshape(n_tiles, tt, d), mod, mod, w_in, cw, cb, lng, lnb, scw, w_out, w1, w2, gf)
    return out.reshape(batch, seq, d)


def kernel(x, c, w_ada, b_ada, w_in, conf_dw_w, conf_dw_b, conf_ln_g, conf_ln_b,
           sc_conv_w, w_out, w_mlp1, w_mlp2, g_final):
    assert w_ada.shape[0] == 1, "fused block implements the depth-1 model"
    batch, _, d = x.shape
    bf16 = jnp.bfloat16
    mod = _adaln_mod(c, w_ada[0], b_ada[0]).reshape(batch, N_MOD, d)
    d_conf, d_sc = conf_dw_w.shape[2], sc_conv_w.shape[2]
    w_val, w_gate, w_b, w_c, w_h = jnp.split(
        w_in[0].astype(bf16), [d_conf, 2 * d_conf, 2 * d_conf + d_sc, 2 * d_conf + 2 * d_sc], axis=1)
    w_in_paired = jnp.concatenate([_pair_columns(w_val, w_gate), _pair_columns(w_c, w_h), w_b], axis=1)
    return _fused_block(
        x, mod, w_in_paired,
        _sublane_replicate(conf_dw_w[0]), _sublane_replicate(conf_dw_b),
        _sublane_replicate(conf_ln_g), _sublane_replicate(conf_ln_b),
        _sublane_replicate(sc_conv_w[0]),
        w_out[0].astype(bf16), w_mlp1[0].astype(bf16), w_mlp2[0].astype(bf16),
        g_final[None])
```

```python
import functools

import jax
import jax.numpy as jnp
from jax import lax
from jax.experimental import pallas as pl
from jax.experimental.pallas import tpu as pltpu

EPS = 1e-6
N_MOD = 6
CONF_KERNEL = 31
SC_KERNEL = 3

SUBLANES_F32 = 8
LANES = 128
VMEM_LIMIT_BYTES = 56 * 1024 * 1024

TIME_TILE = 512
CONV_ROWS = 64
N_CHUNKS = TIME_TILE // CONV_ROWS
HIDDEN_GROUP = 2
AFTER_EVERY = 2
GATE_AHEAD = 2
GATE_BLOCK = 3
A_HALO = -(-(CONF_KERNEL - 1) // SUBLANES_F32) * SUBLANES_F32
Q_HALO = -(-(SC_KERNEL - 1) // SUBLANES_F32) * SUBLANES_F32


def _dot(a, b):
    return jnp.dot(a, b, preferred_element_type=jnp.float32)


def _rms_scale(v):
    return lax.rsqrt(jnp.mean(v * v, axis=-1, keepdims=True) + EPS)


def _rows(tile, rows):
    n = tile.shape[-1]
    return jnp.broadcast_to(tile[None], (rows // SUBLANES_F32, SUBLANES_F32, n)).reshape(rows, n)


def _sublane_replicate(p):
    k, n = p.shape
    return jnp.broadcast_to(p[:, None, :], (k, SUBLANES_F32, n)).reshape(k * SUBLANES_F32, n)


def _pair_columns(w):
    k, n2 = w.shape
    return w.reshape(k, 2, n2 // (2 * LANES), LANES).transpose(0, 2, 1, 3).reshape(k, n2)


def _lane_blocks(n):
    return [slice(lt * LANES, (lt + 1) * LANES) for lt in range(n // LANES)]


def _zero_bits_after(v):
    bits = pltpu.bitcast(v[0:SUBLANES_F32, 0:LANES], jnp.uint32)
    sixteen = jnp.uint32(16)
    return lax.shift_right_logical(lax.shift_right_logical(bits, sixteen), sixteen)


def _order_after(v):
    return pltpu.bitcast(_zero_bits_after(v), jnp.float32)


def _index_after(v):
    return pltpu.bitcast(_zero_bits_after(v), jnp.int32)[0, 0]


def _causal_taps(buf, lt, r0, rows, w_ref, halo, ksize, after=None):
    lanes = slice(lt * LANES, (lt + 1) * LANES)
    acc = None
    for k in range(ksize):
        win = buf[lt, pl.ds(r0 + (halo - (ksize - 1) + k), rows, stride=1), :]
        w = w_ref[SUBLANES_F32 * k:SUBLANES_F32 * (k + 1), lanes]
        if after is not None and k % AFTER_EVERY == 0:
            w = w + after
        term = win * _rows(w, rows)
        acc = term if acc is None else acc + term
    return acc


def _mod_kernel(c_ref, w_ref, b_ref, o_ref):
    o_ref[...] = _dot(jax.nn.silu(c_ref[...]), w_ref[...]) + b_ref[...]


def _adaln_mod(c, w_ada, b_ada):
    batch, d = c.shape
    n = w_ada.shape[1]
    bn = d
    return pl.pallas_call(
        _mod_kernel,
        grid=(n // bn,),
        in_specs=[
            pl.BlockSpec((batch, d), lambda j: (0, 0)),
            pl.BlockSpec((d, bn), lambda j: (0, j)),
            pl.BlockSpec((1, bn), lambda j: (0, j)),
        ],
        out_specs=pl.BlockSpec((batch, bn), lambda j: (0, j)),
        out_shape=jax.ShapeDtypeStruct((batch, n), jnp.float32),
        name="adaln_mod",
    )(c, w_ada, b_ada.reshape(1, n))


def _block_kernel(tiles_per_seq,
                  x_ref, modp_ref, modq_ref, w_in_ref, conf_ref, scw_ref,
                  w_out_ref, w1_ref, w2_ref, gf_ref, o_ref,
                  h_buf, a_buf, q_buf, b_buf, c_buf, mix_buf, x1_buf, h2_buf, u_buf):
    tt, d = h_buf.shape
    d_conf = a_buf.shape[0] * LANES
    d_sc = q_buf.shape[0] * LANES
    bf16 = jnp.bfloat16
    step = pl.program_id(0)
    cb_rows, lng_rows, lnb_rows = (slice(SUBLANES_F32 * (CONF_KERNEL + j), SUBLANES_F32 * (CONF_KERNEL + j + 1))
                                   for j in range(3))
    slot_p = lax.rem(step, 2)
    slot_q = 1 - slot_p

    @pl.when(step == 0)
    def _():
        x1_buf[1] = jnp.zeros((tt, d), jnp.float32)
        h2_buf[1] = jnp.zeros((tt, d), bf16)

    @pl.when(lax.rem(step, tiles_per_seq) == 0)
    def _():
        a_buf[:, 0:A_HALO, :] = jnp.zeros((d_conf // LANES, A_HALO, LANES), jnp.float32)
        q_buf[:, 0:Q_HALO, :] = jnp.zeros((d_sc // LANES, Q_HALO, LANES), jnp.float32)

    modp = modp_ref[0]
    sh1, sc1, g1 = modp[0:1], modp[1:2], modp[2:3]
    sh2, sc2 = modp[3:4], modp[4:5]
    g2 = modq_ref[0][5:6]

    ffc = w1_ref.shape[1] // N_CHUNKS

    def mlp_hidden(i, n, gate):
        lo, hi = i * ffc, (i + n) * ffc
        z = _dot(h2_buf[slot_q + gate], w1_ref[:, lo:hi])
        u_buf[:, lo:hi] = jnp.square(jnp.maximum(z, 0.0)).astype(bf16)
        return [z[:, c * ffc:(c + 1) * ffc] for c in range(n)]

    x = x_ref[0]
    h_buf[...] = ((x * _rms_scale(x)) * (1.0 + sc1) + sh1).astype(bf16)
    proj = _dot(h_buf[...], w_in_ref[...])
    for lt in range(d_conf // LANES):
        lo = 2 * lt * LANES
        a_buf[lt, A_HALO:, :] = (proj[:, lo:lo + LANES]
                                 * jax.nn.sigmoid(proj[:, lo + LANES:lo + 2 * LANES]))
    for lt in range(d_sc // LANES):
        lo = 2 * d_conf + 2 * lt * LANES
        q_buf[lt, Q_HALO:, :] = proj[:, lo:lo + LANES] * proj[:, lo + LANES:lo + 2 * LANES]
    b_buf[...] = proj[:, 2 * d_conf + 2 * d_sc:]

    def row0(i):
        return i * CONV_ROWS if isinstance(i, int) else pl.multiple_of(i * CONV_ROWS, CONV_ROWS)

    def conv_rows(i, z, after):
        r0 = row0(i)
        rows_per_block = tt // (d_conf // LANES)
        for lt, lanes in enumerate(_lane_blocks(d_conf)):
            released = _order_after(z[lt * rows_per_block:, :])
            after = released if after is None else after + released
            taps = _causal_taps(a_buf, lt, r0, CONV_ROWS, conf_ref, A_HALO, CONF_KERNEL, after)
            c_buf[0, pl.ds(r0, CONV_ROWS), lanes] = taps + _rows(conf_ref[cb_rows, lanes], CONV_ROWS)
            after = _order_after(taps)
            if lt == GATE_BLOCK:
                gate = _index_after(taps)
        for lt, lanes in enumerate(_lane_blocks(d_sc)):
            sconv = _causal_taps(q_buf, lt, r0, CONV_ROWS, scw_ref, Q_HALO, SC_KERNEL, after)
            s = b_buf[pl.ds(r0, CONV_ROWS), lanes] * sconv
            mix_buf[pl.ds(r0, CONV_ROWS), d_conf + lt * LANES:d_conf + (lt + 1) * LANES] = s.astype(bf16)
            after = _order_after(sconv)
        return after, gate

    def norm_rows(i, gate):
        r0 = row0(i)
        acc = c_buf[gate, pl.ds(r0, CONV_ROWS), :]
        mu = jnp.mean(acc, axis=-1, keepdims=True)
        dev = acc - mu
        var = jnp.mean(dev * dev, axis=-1, keepdims=True)
        y = (dev * lax.rsqrt(var + EPS) * _rows(conf_ref[lng_rows, :], CONV_ROWS)
             + _rows(conf_ref[lnb_rows, :], CONV_ROWS))
        mix_buf[pl.ds(r0, CONV_ROWS), 0:d_conf] = (y * jax.nn.sigmoid(y)).astype(bf16)

    after = None
    gates = {}
    for i0 in range(0, N_CHUNKS, HIDDEN_GROUP):
        zs = mlp_hidden(i0, HIDDEN_GROUP, gates.get(i0, 0))
        for c, z in enumerate(zs):
            i = i0 + c
            after, gates[i + GATE_AHEAD] = conv_rows(i, z, after)

    a_buf[:, 0:A_HALO, :] = a_buf[:, tt:tt + A_HALO, :]
    q_buf[:, 0:Q_HALO, :] = q_buf[:, tt:tt + Q_HALO, :]

    mlp = _dot(u_buf[...], w2_ref[...])
    x2 = x1_buf[slot_q] + g2 * mlp
    o_ref[0] = (x2 * _rms_scale(x2)) * gf_ref[...]

    mlp_started = _index_after(mlp)
    for i in range(N_CHUNKS):
        norm_rows(i, mlp_started)

    x1 = x_ref[0] + g1 * _dot(mix_buf[...], w_out_ref[...])
    x1_buf[slot_p] = x1
    h2_buf[slot_p] = ((x1 * _rms_scale(x1)) * (1.0 + sc2) + sh2).astype(bf16)


def _resident(shape):
    return pl.BlockSpec(shape, lambda s: (0,) * len(shape), pipeline_mode=pl.Buffered(1))


def _fused_block(x, mod, w_in, conf, scw, w_out, w1, w2, gf):
    batch, seq, d = x.shape
    d_conf = conf.shape[1]
    d_sc = scw.shape[1]
    d_ff = w1.shape[1]
    tt = TIME_TILE
    assert seq % tt == 0 and d_ff % N_CHUNKS == 0 and N_CHUNKS % HIDDEN_GROUP == 0
    assert d_conf % LANES == 0 and d_sc % LANES == 0
    tiles_per_seq = seq // tt
    n_tiles = batch * tiles_per_seq
    bf16 = jnp.bfloat16

    def tile_p(s):
        return jnp.minimum(s, n_tiles - 1)

    def tile_q(s):
        return jnp.maximum(s - 1, 0)

    out = pl.pallas_call(
        functools.partial(_block_kernel, tiles_per_seq),
        grid=(n_tiles + 1,),
        in_specs=[
            pl.BlockSpec((1, tt, d), lambda s: (tile_p(s), 0, 0)),
            pl.BlockSpec((1, N_MOD, d), lambda s: (tile_p(s) // tiles_per_seq, 0, 0)),
            pl.BlockSpec((1, N_MOD, d), lambda s: (tile_q(s) // tiles_per_seq, 0, 0)),
            _resident(w_in.shape),
            _resident(conf.shape),
            _resident(scw.shape),
            _resident(w_out.shape),
            _resident(w1.shape),
            _resident(w2.shape),
            _resident(gf.shape),
        ],
        out_specs=pl.BlockSpec((1, tt, d), lambda s: (tile_q(s), 0, 0)),
        out_shape=jax.ShapeDtypeStruct((n_tiles, tt, d), x.dtype),
        scratch_shapes=[
            pltpu.VMEM((tt, d), bf16),
            pltpu.VMEM((d_conf // LANES, A_HALO + tt, LANES), jnp.float32),
            pltpu.VMEM((d_sc // LANES, Q_HALO + tt, LANES), jnp.float32),
            pltpu.VMEM((tt, d_sc), jnp.float32),
            pltpu.VMEM((1, tt, d_conf), jnp.float32),
            pltpu.VMEM((tt, d), bf16),
            pltpu.VMEM((2, tt, d), jnp.float32),
            pltpu.VMEM((2, tt, d), bf16),
            pltpu.VMEM((tt, d_ff), bf16),
        ],
        compiler_params=pltpu.CompilerParams(
            dimension_semantics=("arbitrary",),
            vmem_limit_bytes=VMEM_LIMIT_BYTES,
        ),
        name="fused_block",
    )(x.reshape(n_tiles, tt, d), mod, mod, w_in, conf, scw, w_out, w1, w2, gf)
    return out.reshape(batch, seq, d)


def kernel(x, c, w_ada, b_ada, w_in, conf_dw_w, conf_dw_b, conf_ln_g, conf_ln_b,
           sc_conv_w, w_out, w_mlp1, w_mlp2, g_final):
    assert w_ada.shape[0] == 1, "fused block implements the depth-1 model"
    batch, _, d = x.shape
    bf16 = jnp.bfloat16
    mod = _adaln_mod(c, w_ada[0], b_ada[0]).reshape(batch, N_MOD, d)
    d_conf, d_sc = conf_dw_w.shape[2], sc_conv_w.shape[2]
    w = w_in[0]
    w_in_paired = jnp.concatenate(
        [_pair_columns(w[:, :2 * d_conf]), _pair_columns(w[:, 2 * d_conf + d_sc:]),
         w[:, 2 * d_conf:2 * d_conf + d_sc]], axis=1).astype(bf16)
    conf = _sublane_replicate(jnp.concatenate([conf_dw_w[0], conf_dw_b, conf_ln_g, conf_ln_b], axis=0))
    return _fused_block(
        x, mod, w_in_paired, conf, _sublane_replicate(sc_conv_w[0]),
        w_out[0].astype(bf16), w_mlp1[0].astype(bf16), w_mlp2[0].astype(bf16), g_final[None])
```

```python
import functools

import jax
import jax.numpy as jnp
from jax import lax
from jax.experimental import pallas as pl
from jax.experimental.pallas import tpu as pltpu

EPS = 1e-6
N_MOD = 6
CONF_KERNEL = 31
SC_KERNEL = 3

SUBLANES_F32 = 8
LANES = 128
VMEM_LIMIT_BYTES = 56 * 1024 * 1024

TIME_TILE = 512
CONV_ROWS = 64
N_CHUNKS = TIME_TILE // CONV_ROWS
HIDDEN_GROUP = 2
AFTER_EVERY = 1
GATE_AHEAD = 2
GATE_BLOCK = 3
A_HALO = -(-(CONF_KERNEL - 1) // SUBLANES_F32) * SUBLANES_F32
Q_HALO = -(-(SC_KERNEL - 1) // SUBLANES_F32) * SUBLANES_F32


def _dot(a, b):
    return jnp.dot(a, b, preferred_element_type=jnp.float32)


def _rms_scale(v):
    return lax.rsqrt(jnp.mean(v * v, axis=-1, keepdims=True) + EPS)


def _rows(tile, rows):
    n = tile.shape[-1]
    return jnp.broadcast_to(tile[None], (rows // SUBLANES_F32, SUBLANES_F32, n)).reshape(rows, n)


def _sublane_replicate(p):
    k, n = p.shape
    return jnp.broadcast_to(p[:, None, :], (k, SUBLANES_F32, n)).reshape(k * SUBLANES_F32, n)


def _pair_columns(w):
    k, n2 = w.shape
    return w.reshape(k, 2, n2 // (2 * LANES), LANES).transpose(0, 2, 1, 3).reshape(k, n2)


def _lane_blocks(n):
    return [slice(lt * LANES, (lt + 1) * LANES) for lt in range(n // LANES)]


def _zero_bits_after(v):
    bits = pltpu.bitcast(v[0:SUBLANES_F32, 0:LANES], jnp.uint32)
    sixteen = jnp.uint32(16)
    return lax.shift_right_logical(lax.shift_right_logical(bits, sixteen), sixteen)


def _order_after(v):
    return pltpu.bitcast(_zero_bits_after(v), jnp.float32)


def _index_after(v):
    return pltpu.bitcast(_zero_bits_after(v), jnp.int32)[0, 0]


def _causal_taps(buf, lt, r0, rows, w_ref, halo, ksize, after=None):
    lanes = slice(lt * LANES, (lt + 1) * LANES)
    acc = None
    for k in range(ksize):
        win = buf[lt, pl.ds(r0 + (halo - (ksize - 1) + k), rows, stride=1), :]
        w = w_ref[SUBLANES_F32 * k:SUBLANES_F32 * (k + 1), lanes]
        if after is not None and k % AFTER_EVERY == 0:
            w = w + after
        term = win * _rows(w, rows)
        acc = term if acc is None else acc + term
    return acc


def _mod_kernel(c_ref, w_ref, b_ref, o_ref):
    o_ref[...] = _dot(jax.nn.silu(c_ref[...]), w_ref[...]) + b_ref[...]


def _adaln_mod(c, w_ada, b_ada):
    batch, d = c.shape
    n = w_ada.shape[1]
    bn = d
    return pl.pallas_call(
        _mod_kernel,
        grid=(n // bn,),
        in_specs=[
            pl.BlockSpec((batch, d), lambda j: (0, 0)),
            pl.BlockSpec((d, bn), lambda j: (0, j)),
            pl.BlockSpec((1, bn), lambda j: (0, j)),
        ],
        out_specs=pl.BlockSpec((batch, bn), lambda j: (0, j)),
        out_shape=jax.ShapeDtypeStruct((batch, n), jnp.float32),
        name="adaln_mod",
    )(c, w_ada, b_ada.reshape(1, n))


def _block_kernel(tiles_per_seq,
                  x_ref, modp_ref, modq_ref, w_in_ref, conf_ref, scw_ref,
                  w_out_ref, w1_ref, w2_ref, gf_ref, o_ref,
                  h_buf, a_buf, q_buf, b_buf, c_buf, mix_buf, x1_buf, h2_buf, u_buf):
    tt, d = h_buf.shape
    d_conf = a_buf.shape[0] * LANES
    d_sc = q_buf.shape[0] * LANES
    bf16 = jnp.bfloat16
    step = pl.program_id(0)
    cb_rows, lng_rows, lnb_rows = (slice(SUBLANES_F32 * (CONF_KERNEL + j), SUBLANES_F32 * (CONF_KERNEL + j + 1))
                                   for j in range(3))
    slot_p = lax.rem(step, 2)
    slot_q = 1 - slot_p

    @pl.when(step == 0)
    def _():
        x1_buf[1] = jnp.zeros((tt, d), jnp.float32)
        h2_buf[1] = jnp.zeros((tt, d), bf16)

    @pl.when(lax.rem(step, tiles_per_seq) == 0)
    def _():
        a_buf[:, 0:A_HALO, :] = jnp.zeros((d_conf // LANES, A_HALO, LANES), jnp.float32)
        q_buf[:, 0:Q_HALO, :] = jnp.zeros((d_sc // LANES, Q_HALO, LANES), jnp.float32)

    modp = modp_ref[0]
    sh1, sc1, g1 = modp[0:1], modp[1:2], modp[2:3]
    sh2, sc2 = modp[3:4], modp[4:5]
    g2 = modq_ref[0][5:6]

    ffc = w1_ref.shape[1] // N_CHUNKS

    def mlp_hidden(i, n, gate):
        lo, hi = i * ffc, (i + n) * ffc
        z = _dot(h2_buf[slot_q + gate], w1_ref[:, lo:hi])
        u_buf[:, lo:hi] = jnp.square(jnp.maximum(z, 0.0)).astype(bf16)
        return [z[:, c * ffc:(c + 1) * ffc] for c in range(n)]

    x = x_ref[0]
    h_buf[...] = ((x * _rms_scale(x)) * (1.0 + sc1) + sh1).astype(bf16)
    proj = _dot(h_buf[...], w_in_ref[...])
    for lt in range(d_conf // LANES):
        lo = 2 * lt * LANES
        a_buf[lt, A_HALO:, :] = (proj[:, lo:lo + LANES]
                                 * jax.nn.sigmoid(proj[:, lo + LANES:lo + 2 * LANES]))
    for lt in range(d_sc // LANES):
        lo = 2 * d_conf + 2 * lt * LANES
        q_buf[lt, Q_HALO:, :] = proj[:, lo:lo + LANES] * proj[:, lo + LANES:lo + 2 * LANES]
    b_buf[...] = proj[:, 2 * d_conf + 2 * d_sc:]

    def row0(i):
        return i * CONV_ROWS if isinstance(i, int) else pl.multiple_of(i * CONV_ROWS, CONV_ROWS)

    def conv_rows(i, z, after):
        r0 = row0(i)
        rows_per_block = tt // (d_conf // LANES)
        for lt, lanes in enumerate(_lane_blocks(d_conf)):
            released = _order_after(z[lt * rows_per_block:, :])
            after = released if after is None else after + released
            taps = _causal_taps(a_buf, lt, r0, CONV_ROWS, conf_ref, A_HALO, CONF_KERNEL, after)
            c_buf[0, pl.ds(r0, CONV_ROWS), lanes] = taps + _rows(conf_ref[cb_rows, lanes], CONV_ROWS)
            after = _order_after(taps)
            if lt == GATE_BLOCK:
                gate = _index_after(taps)
        for lt, lanes in enumerate(_lane_blocks(d_sc)):
            sconv = _causal_taps(q_buf, lt, r0, CONV_ROWS, scw_ref, Q_HALO, SC_KERNEL, after)
            s = b_buf[pl.ds(r0, CONV_ROWS), lanes] * sconv
            mix_buf[pl.ds(r0, CONV_ROWS), d_conf + lt * LANES:d_conf + (lt + 1) * LANES] = s.astype(bf16)
            after = _order_after(sconv)
        return after, gate

    def norm_rows(i, gate):
        r0 = row0(i)
        acc = c_buf[gate, pl.ds(r0, CONV_ROWS), :]
        mu = jnp.mean(acc, axis=-1, keepdims=True)
        dev = acc - mu
        var = jnp.mean(dev * dev, axis=-1, keepdims=True)
        y = (dev * lax.rsqrt(var + EPS) * _rows(conf_ref[lng_rows, :], CONV_ROWS)
             + _rows(conf_ref[lnb_rows, :], CONV_ROWS))
        mix_buf[pl.ds(r0, CONV_ROWS), 0:d_conf] = (y * jax.nn.sigmoid(y)).astype(bf16)

    after = None
    gates = {}
    for i0 in range(0, N_CHUNKS, HIDDEN_GROUP):
        zs = mlp_hidden(i0, HIDDEN_GROUP, gates.get(i0, 0))
        for c, z in enumerate(zs):
            i = i0 + c
            after, gates[i + GATE_AHEAD] = conv_rows(i, z, after)

    a_buf[:, 0:A_HALO, :] = a_buf[:, tt:tt + A_HALO, :]
    q_buf[:, 0:Q_HALO, :] = q_buf[:, tt:tt + Q_HALO, :]

    mlp = _dot(u_buf[...], w2_ref[...])
    x2 = x1_buf[slot_q] + g2 * mlp
    o_ref[0] = (x2 * _rms_scale(x2)) * gf_ref[...]

    mlp_started = _index_after(mlp)
    for i in range(N_CHUNKS):
        norm_rows(i, mlp_started)

    x1 = x_ref[0] + g1 * _dot(mix_buf[...], w_out_ref[...])
    x1_buf[slot_p] = x1
    h2_buf[slot_p] = ((x1 * _rms_scale(x1)) * (1.0 + sc2) + sh2).astype(bf16)


def _resident(shape):
    return pl.BlockSpec(shape, lambda s: (0,) * len(shape), pipeline_mode=pl.Buffered(1))


def _fused_block(x, mod, w_in, conf, scw, w_out, w1, w2, gf):
    batch, seq, d = x.shape
    d_conf = conf.shape[1]
    d_sc = scw.shape[1]
    d_ff = w1.shape[1]
    tt = TIME_TILE
    assert seq % tt == 0 and d_ff % N_CHUNKS == 0 and N_CHUNKS % HIDDEN_GROUP == 0
    assert d_conf % LANES == 0 and d_sc % LANES == 0
    tiles_per_seq = seq // tt
    n_tiles = batch * tiles_per_seq
    bf16 = jnp.bfloat16

    def tile_p(s):
        return jnp.minimum(s, n_tiles - 1)

    def tile_q(s):
        return jnp.maximum(s - 1, 0)

    out = pl.pallas_call(
        functools.partial(_block_kernel, tiles_per_seq),
        grid=(n_tiles + 1,),
        in_specs=[
            pl.BlockSpec((1, tt, d), lambda s: (tile_p(s), 0, 0)),
            pl.BlockSpec((1, N_MOD, d), lambda s: (tile_p(s) // tiles_per_seq, 0, 0)),
            pl.BlockSpec((1, N_MOD, d), lambda s: (tile_q(s) // tiles_per_seq, 0, 0)),
            _resident(w_in.shape),
            _resident(conf.shape),
            _resident(scw.shape),
            _resident(w_out.shape),
            _resident(w1.shape),
            _resident(w2.shape),
            _resident(gf.shape),
        ],
        out_specs=pl.BlockSpec((1, tt, d), lambda s: (tile_q(s), 0, 0)),
        out_shape=jax.ShapeDtypeStruct((n_tiles, tt, d), x.dtype),
        scratch_shapes=[
            pltpu.VMEM((tt, d), bf16),
            pltpu.VMEM((d_conf // LANES, A_HALO + tt, LANES), jnp.float32),
            pltpu.VMEM((d_sc // LANES, Q_HALO + tt, LANES), jnp.float32),
            pltpu.VMEM((tt, d_sc), jnp.float32),
            pltpu.VMEM((1, tt, d_conf), jnp.float32),
            pltpu.VMEM((tt, d), bf16),
            pltpu.VMEM((2, tt, d), jnp.float32),
            pltpu.VMEM((2, tt, d), bf16),
            pltpu.VMEM((tt, d_ff), bf16),
        ],
        compiler_params=pltpu.CompilerParams(
            dimension_semantics=("arbitrary",),
            vmem_limit_bytes=VMEM_LIMIT_BYTES,
        ),
        name="fused_block",
    )(x.reshape(n_tiles, tt, d), mod, mod, w_in, conf, scw, w_out, w1, w2, gf)
    return out.reshape(batch, seq, d)


def kernel(x, c, w_ada, b_ada, w_in, conf_dw_w, conf_dw_b, conf_ln_g, conf_ln_b,
           sc_conv_w, w_out, w_mlp1, w_mlp2, g_final):
    assert w_ada.shape[0] == 1, "fused block implements the depth-1 model"
    batch, _, d = x.shape
    bf16 = jnp.bfloat16
    mod = _adaln_mod(c, w_ada[0], b_ada[0]).reshape(batch, N_MOD, d)
    d_conf, d_sc = conf_dw_w.shape[2], sc_conv_w.shape[2]
    w = w_in[0]
    w_in_paired = jnp.concatenate(
        [_pair_columns(w[:, :2 * d_conf]), _pair_columns(w[:, 2 * d_conf + d_sc:]),
         w[:, 2 * d_conf:2 * d_conf + d_sc]], axis=1).astype(bf16)
    conf = _sublane_replicate(jnp.concatenate([conf_dw_w[0], conf_dw_b, conf_ln_g, conf_ln_b], axis=0))
    return _fused_block(
        x, mod, w_in_paired, conf, _sublane_replicate(sc_conv_w[0]),
        w_out[0].astype(bf16), w_mlp1[0].astype(bf16), w_mlp2[0].astype(bf16), g_final[None])
```

```python
import functools

import jax
import jax.numpy as jnp
from jax import lax
from jax.experimental import pallas as pl
from jax.experimental.pallas import tpu as pltpu

EPS = 1e-6
N_MOD = 6
CONF_KERNEL = 31
SC_KERNEL = 3

SUBLANES_F32 = 8
LANES = 128
VMEM_LIMIT_BYTES = 56 * 1024 * 1024

TIME_TILE = 512
CONV_ROWS = 64
N_CHUNKS = TIME_TILE // CONV_ROWS
HIDDEN_GROUP = 2
AFTER_EVERY = 1
GATE_AHEAD = 2
GATE_BLOCK = 3
A_HALO = -(-(CONF_KERNEL - 1) // SUBLANES_F32) * SUBLANES_F32
Q_HALO = -(-(SC_KERNEL - 1) // SUBLANES_F32) * SUBLANES_F32


def _dot(a, b):
    return jnp.dot(a, b, preferred_element_type=jnp.float32)


def _rms_scale(v):
    return lax.rsqrt(jnp.mean(v * v, axis=-1, keepdims=True) + EPS)


def _rows(tile, rows):
    n = tile.shape[-1]
    return jnp.broadcast_to(tile[None], (rows // SUBLANES_F32, SUBLANES_F32, n)).reshape(rows, n)


def _sublane_replicate(p):
    k, n = p.shape
    return jnp.broadcast_to(p[:, None, :], (k, SUBLANES_F32, n)).reshape(k * SUBLANES_F32, n)


def _pair_columns(w):
    k, n2 = w.shape
    return w.reshape(k, 2, n2 // (2 * LANES), LANES).transpose(0, 2, 1, 3).reshape(k, n2)


def _lane_blocks(n):
    return [slice(lt * LANES, (lt + 1) * LANES) for lt in range(n // LANES)]


def _zero_bits_after(v):
    bits = pltpu.bitcast(v[0:SUBLANES_F32, 0:LANES], jnp.uint32)
    sixteen = jnp.uint32(16)
    return lax.shift_right_logical(lax.shift_right_logical(bits, sixteen), sixteen)


def _order_after(v):
    return pltpu.bitcast(_zero_bits_after(v), jnp.float32)


def _index_after(v):
    return pltpu.bitcast(_zero_bits_after(v), jnp.int32)[0, 0]


def _causal_taps(buf, lt, r0, rows, w_ref, halo, ksize, after=None):
    lanes = slice(lt * LANES, (lt + 1) * LANES)
    acc = None
    for k in range(ksize):
        win = buf[lt, pl.ds(r0 + (halo - (ksize - 1) + k), rows, stride=1), :]
        w = w_ref[SUBLANES_F32 * k:SUBLANES_F32 * (k + 1), lanes]
        if after is not None and k % AFTER_EVERY == 0:
            w = w + after
        term = win * _rows(w, rows)
        acc = term if acc is None else acc + term
    return acc


def _mod_kernel(c_ref, w_ref, b_ref, o_ref):
    o_ref[...] = _dot(jax.nn.silu(c_ref[...]), w_ref[...]) + b_ref[...]


def _adaln_mod(c, w_ada, b_ada):
    batch, d = c.shape
    n = w_ada.shape[1]
    bn = d
    return pl.pallas_call(
        _mod_kernel,
        grid=(n // bn,),
        in_specs=[
            pl.BlockSpec((batch, d), lambda j: (0, 0)),
            pl.BlockSpec((d, bn), lambda j: (0, j)),
            pl.BlockSpec((1, bn), lambda j: (0, j)),
        ],
        out_specs=pl.BlockSpec((batch, bn), lambda j: (0, j)),
        out_shape=jax.ShapeDtypeStruct((batch, n), jnp.float32),
        name="adaln_mod",
    )(c, w_ada, b_ada.reshape(1, n))


def _block_kernel(tiles_per_seq,
                  x_ref, modp_ref, modq_ref, w_in_ref, conf_ref, scw_ref,
                  w_out_ref, w1_ref, w2_ref, gf_ref, o_ref,
                  h_buf, a_buf, q_buf, b_buf, c_buf, mix_buf, x1_buf, h2_buf, u_buf):
    tt, d = h_buf.shape
    d_conf = a_buf.shape[0] * LANES
    d_sc = q_buf.shape[0] * LANES
    bf16 = jnp.bfloat16
    step = pl.program_id(0)
    cb_rows, lng_rows, lnb_rows = (slice(SUBLANES_F32 * (CONF_KERNEL + j), SUBLANES_F32 * (CONF_KERNEL + j + 1))
                                   for j in range(3))
    slot_p = lax.rem(step, 2)
    slot_q = 1 - slot_p

    @pl.when(step == 0)
    def _():
        x1_buf[1] = jnp.zeros((tt, d), jnp.float32)
        h2_buf[1] = jnp.zeros((tt, d), bf16)

    @pl.when(lax.rem(step, tiles_per_seq) == 0)
    def _():
        a_buf[:, 0:A_HALO, :] = jnp.zeros((d_conf // LANES, A_HALO, LANES), jnp.float32)
        q_buf[:, 0:Q_HALO, :] = jnp.zeros((d_sc // LANES, Q_HALO, LANES), jnp.float32)

    modp = modp_ref[0]
    sh1, sc1, g1 = modp[0:1], modp[1:2], modp[2:3]
    sh2, sc2 = modp[3:4], modp[4:5]
    g2 = modq_ref[0][5:6]

    ffc = w1_ref.shape[1] // N_CHUNKS

    def mlp_hidden(i, n, gate):
        lo, hi = i * ffc, (i + n) * ffc
        z = _dot(h2_buf[slot_q + gate], w1_ref[:, lo:hi])
        u_buf[:, lo:hi] = jnp.square(jnp.maximum(z, 0.0)).astype(bf16)
        return [z[:, c * ffc:(c + 1) * ffc] for c in range(n)]

    x = x_ref[0]
    h_buf[...] = ((x * _rms_scale(x)) * (1.0 + sc1) + sh1).astype(bf16)
    proj = _dot(h_buf[...], w_in_ref[...])
    for lt in range(d_conf // LANES):
        lo = 2 * lt * LANES
        a_buf[lt, A_HALO:, :] = (proj[:, lo:lo + LANES]
                                 * jax.nn.sigmoid(proj[:, lo + LANES:lo + 2 * LANES]))
    for lt in range(d_sc // LANES):
        lo = 2 * d_conf + 2 * lt * LANES
        q_buf[lt, Q_HALO:, :] = proj[:, lo:lo + LANES] * proj[:, lo + LANES:lo + 2 * LANES]
    b_buf[...] = proj[:, 2 * d_conf + 2 * d_sc:]

    def row0(i):
        return i * CONV_ROWS if isinstance(i, int) else pl.multiple_of(i * CONV_ROWS, CONV_ROWS)

    def conv_rows(i, z, after):
        r0 = row0(i)
        rows_per_block = tt // (d_conf // LANES)
        for lt, lanes in enumerate(_lane_blocks(d_conf)):
            released = _order_after(z[lt * rows_per_block:, :])
            after = released if after is None else after + released
            taps = _causal_taps(a_buf, lt, r0, CONV_ROWS, conf_ref, A_HALO, CONF_KERNEL, after)
            c_buf[0, pl.ds(r0, CONV_ROWS), lanes] = taps + _rows(conf_ref[cb_rows, lanes], CONV_ROWS)
            after = _order_after(taps)
            if lt == GATE_BLOCK:
                gate = _index_after(taps)
        for lt, lanes in enumerate(_lane_blocks(d_sc)):
            sconv = _causal_taps(q_buf, lt, r0, CONV_ROWS, scw_ref, Q_HALO, SC_KERNEL, after)
            s = b_buf[pl.ds(r0, CONV_ROWS), lanes] * sconv
            mix_buf[pl.ds(r0, CONV_ROWS), d_conf + lt * LANES:d_conf + (lt + 1) * LANES] = s.astype(bf16)
            after = _order_after(sconv)
        return after, gate

    def norm_rows(i, gate):
        r0 = row0(i)
        acc = c_buf[gate, pl.ds(r0, CONV_ROWS), :]
        mu = jnp.mean(acc, axis=-1, keepdims=True)
        dev = acc - mu
        var = jnp.mean(dev * dev, axis=-1, keepdims=True)
        y = (dev * lax.rsqrt(var + EPS) * _rows(conf_ref[lng_rows, :], CONV_ROWS)
             + _rows(conf_ref[lnb_rows, :], CONV_ROWS))
        mix_buf[pl.ds(r0, CONV_ROWS), 0:d_conf] = (y * jax.nn.sigmoid(y)).astype(bf16)

    after = None
    gates = {}
    for i0 in range(0, N_CHUNKS, HIDDEN_GROUP):
        zs = mlp_hidden(i0, HIDDEN_GROUP, 0)
        for c, z in enumerate(zs):
            i = i0 + c
            after, gates[i + GATE_AHEAD] = conv_rows(i, z, after)

    a_buf[:, 0:A_HALO, :] = a_buf[:, tt:tt + A_HALO, :]
    q_buf[:, 0:Q_HALO, :] = q_buf[:, tt:tt + Q_HALO, :]

    mlp = _dot(u_buf[...], w2_ref[...])
    x2 = x1_buf[slot_q] + g2 * mlp
    o_ref[0] = (x2 * _rms_scale(x2)) * gf_ref[...]

    mlp_started = _index_after(mlp)
    for i in range(N_CHUNKS):
        norm_rows(i, mlp_started)

    x1 = x_ref[0] + g1 * _dot(mix_buf[...], w_out_ref[...])
    x1_buf[slot_p] = x1
    h2_buf[slot_p] = ((x1 * _rms_scale(x1)) * (1.0 + sc2) + sh2).astype(bf16)


def _resident(shape):
    return pl.BlockSpec(shape, lambda s: (0,) * len(shape), pipeline_mode=pl.Buffered(1))


def _fused_block(x, mod, w_in, conf, scw, w_out, w1, w2, gf):
    batch, seq, d = x.shape
    d_conf = conf.shape[1]
    d_sc = scw.shape[1]
    d_ff = w1.shape[1]
    tt = TIME_TILE
    assert seq % tt == 0 and d_ff % N_CHUNKS == 0 and N_CHUNKS % HIDDEN_GROUP == 0
    assert d_conf % LANES == 0 and d_sc % LANES == 0
    tiles_per_seq = seq // tt
    n_tiles = batch * tiles_per_seq
    bf16 = jnp.bfloat16

    def tile_p(s):
        return jnp.minimum(s, n_tiles - 1)

    def tile_q(s):
        return jnp.maximum(s - 1, 0)

    out = pl.pallas_call(
        functools.partial(_block_kernel, tiles_per_seq),
        grid=(n_tiles + 1,),
        in_specs=[
            pl.BlockSpec((1, tt, d), lambda s: (tile_p(s), 0, 0)),
            pl.BlockSpec((1, N_MOD, d), lambda s: (tile_p(s) // tiles_per_seq, 0, 0)),
            pl.BlockSpec((1, N_MOD, d), lambda s: (tile_q(s) // tiles_per_seq, 0, 0)),
            _resident(w_in.shape),
            _resident(conf.shape),
            _resident(scw.shape),
            _resident(w_out.shape),
            _resident(w1.shape),
            _resident(w2.shape),
            _resident(gf.shape),
        ],
        out_specs=pl.BlockSpec((1, tt, d), lambda s: (tile_q(s), 0, 0)),
        out_shape=jax.ShapeDtypeStruct((n_tiles, tt, d), x.dtype),
        scratch_shapes=[
            pltpu.VMEM((tt, d), bf16),
            pltpu.VMEM((d_conf // LANES, A_HALO + tt, LANES), jnp.float32),
            pltpu.VMEM((d_sc // LANES, Q_HALO + tt, LANES), jnp.float32),
            pltpu.VMEM((tt, d_sc), jnp.float32),
            pltpu.VMEM((1, tt, d_conf), jnp.float32),
            pltpu.VMEM((tt, d), bf16),
            pltpu.VMEM((2, tt, d), jnp.float32),
            pltpu.VMEM((2, tt, d), bf16),
            pltpu.VMEM((tt, d_ff), bf16),
        ],
        compiler_params=pltpu.CompilerParams(
            dimension_semantics=("arbitrary",),
            vmem_limit_bytes=VMEM_LIMIT_BYTES,
        ),
        name="fused_block",
    )(x.reshape(n_tiles, tt, d), mod, mod, w_in, conf, scw, w_out, w1, w2, gf)
    return out.reshape(batch, seq, d)


def kernel(x, c, w_ada, b_ada, w_in, conf_dw_w, conf_dw_b, conf_ln_g, conf_ln_b,
           sc_conv_w, w_out, w_mlp1, w_mlp2, g_final):
    assert w_ada.shape[0] == 1, "fused block implements the depth-1 model"
    batch, _, d = x.shape
    bf16 = jnp.bfloat16
    mod = _adaln_mod(c, w_ada[0], b_ada[0]).reshape(batch, N_MOD, d)
    d_conf, d_sc = conf_dw_w.shape[2], sc_conv_w.shape[2]
    w = w_in[0]
    w_in_paired = jnp.concatenate(
        [_pair_columns(w[:, :2 * d_conf]), _pair_columns(w[:, 2 * d_conf + d_sc:]),
         w[:, 2 * d_conf:2 * d_conf + d_sc]], axis=1).astype(bf16)
    conf = _sublane_replicate(jnp.concatenate([conf_dw_w[0], conf_dw_b, conf_ln_g, conf_ln_b], axis=0))
    return _fused_block(
        x, mod, w_in_paired, conf, _sublane_replicate(sc_conv_w[0]),
        w_out[0].astype(bf16), w_mlp1[0].astype(bf16), w_mlp2[0].astype(bf16), g_final[None])
```
